```python
import math
import jax
import jax.numpy as jnp
from jax import lax
import numpy as np

D_MODEL = 1024
BATCH = 16
SEQ = 2048
DEPTH = 2

GRID_W = 64
CTX_LEN = 256
NORM_EPS = 1e-6
NA_HEADS = 4
NA_HEAD_DIM = 64
NA_ROWS = 8
NA_COLS = 16
NA_QCOLS = 16
NA_KCOLS = NA_QCOLS + NA_COLS
NA_WIDTH = NA_HEADS * NA_HEAD_DIM
DF_HEADS = 4
DF_HEAD_DIM = 32
DF_WIDTH = DF_HEADS * 2 * DF_HEAD_DIM
DF_Q_BLOCK = 128
ROPE_BASE = 10000.0
SSM_HEADS = 8
SSM_HEAD_DIM = 64
SSM_D_INNER = SSM_HEADS * SSM_HEAD_DIM
SSM_GROUPS = 2
SSM_STATE = 128
SSM_CONV = 5
SSM_CHUNK = 128
SSM_CONV_DIM = SSM_D_INNER + 2 * SSM_GROUPS * SSM_STATE
D_MIX = NA_WIDTH + DF_WIDTH + SSM_D_INNER
IN_SPLITS = (NA_WIDTH, NA_WIDTH, NA_WIDTH, DF_WIDTH, DF_WIDTH, DF_WIDTH, SSM_D_INNER, SSM_CONV_DIM, 2 * SSM_HEADS)
IN_COLS = sum(IN_SPLITS)
D_FF = 2816
FFN_CONV = 3

kernel_name = 'hymba_style_na_diff_ssd_block'


def rms_norm(x, gain):
    xf = x.astype(jnp.float32)
    y = xf * lax.rsqrt(jnp.mean(xf * xf, axis=-1, keepdims=True) + NORM_EPS)
    return (y * gain.astype(jnp.float32)).astype(x.dtype)


def depthwise_conv(x, w, b):
    y = lax.conv_general_dilated(x, w[:, None, :].astype(x.dtype), window_strides=(1,), padding='SAME',
                                 dimension_numbers=('NWC', 'WIO', 'NWC'), feature_group_count=x.shape[-1])
    return y + b


def axial_rope_tables(n, dim):
    per_axis = dim // 2
    inv_freq = ROPE_BASE ** (-jnp.arange(0, per_axis, 2, dtype=jnp.float32) / per_axis)
    t = jnp.arange(n, dtype=jnp.int32)
    pos = jnp.stack([t // GRID_W, t % GRID_W], axis=-1).astype(jnp.float32)
    ang = pos[:, :, None] * inv_freq
    ang = jnp.concatenate([ang, ang], axis=-1).reshape(n, dim)
    return jnp.cos(ang), jnp.sin(ang)


def apply_axial_rope(x, cos, sin):
    k = x.shape[-1] // 4
    xf = x.astype(jnp.float32)
    xr = xf.reshape(*x.shape[:-1], 2, 2, k)
    rot = jnp.concatenate([-xr[..., 1:, :], xr[..., :1, :]], axis=-2).reshape(x.shape)
    shape = (x.shape[1],) + (1,) * (x.ndim - 3) + (x.shape[-1],)
    return (xf * cos.reshape(shape) + rot * sin.reshape(shape)).astype(x.dtype)


def dense_attention(q, k, v):
    s = jnp.einsum('bqhd,bkhd->bhqk', q, k).astype(jnp.float32) * q.shape[-1] ** -0.5
    p = jax.nn.softmax(s, axis=-1).astype(v.dtype)
    return jnp.einsum('bhqk,bkhd->bqhd', p, v)


def neighbourhood_attention(q, k, v, k_ctx, v_ctx, rpb):
    b, n, h, d = q.shape
    rows = n // GRID_W
    wr = min(NA_ROWS, rows)
    ncb = GRID_W // NA_QCOLS
    r = np.arange(rows)
    key_rows = np.clip(r - wr // 2, 0, rows - wr)[:, None] + np.arange(wr)
    jb = np.arange(ncb)
    key_cols = np.clip(jb * NA_QCOLS - NA_COLS // 2, 0, GRID_W - NA_KCOLS)[:, None] + np.arange(NA_KCOLS)
    q_cols = jb[:, None] * NA_QCOLS + np.arange(NA_QCOLS)
    win_start = np.clip(q_cols - NA_COLS // 2, 0, GRID_W - NA_COLS)
    col_ok = (key_cols[:, None, :] >= win_start[..., None]) & (key_cols[:, None, :] < win_start[..., None] + NA_COLS)
    nk = wr * NA_KCOLS
    idx = (key_rows[:, None, :, None] * GRID_W + key_cols[None, :, None, :]).reshape(rows, ncb, nk)
    dr = key_rows - r[:, None] + NA_ROWS - 1
    dc = np.clip(key_cols[:, None, :] - q_cols[:, :, None], 1 - NA_COLS, NA_COLS - 1) + NA_COLS - 1
    bias = rpb[:, dr[:, None, None, :, None], dc[None, :, :, None, :]].astype(jnp.float32)
    bias = jnp.where(col_ok[None, None, :, :, None, :], bias, -jnp.inf).reshape(h, rows, ncb, NA_QCOLS, nk)
    qb = q.reshape(b, rows, ncb, NA_QCOLS, h, d)
    kg, vg = k[:, idx], v[:, idx]
    scale = d ** -0.5
    s_loc = jnp.einsum('brjqhd,brjkhd->bhrjqk', qb, kg).astype(jnp.float32) * scale + bias
    s_ctx = jnp.einsum('brjqhd,bkhd->bhrjqk', qb, k_ctx).astype(jnp.float32) * scale
    p = jax.nn.softmax(jnp.concatenate([s_loc, s_ctx], axis=-1), axis=-1).astype(v.dtype)
    out = (jnp.einsum('bhrjqk,brjkhd->brjqhd', p[..., :nk], vg)
           + jnp.einsum('bhrjqk,bkhd->brjqhd', p[..., nk:], v_ctx))
    return out.reshape(b, n, h, d)


def diff_attend(q, k, v, lam):
    s = jnp.einsum('bqhmd,bkhmd->bhmqk', q, k).astype(jnp.float32) * DF_HEAD_DIM ** -0.5
    p = jax.nn.softmax(s, axis=-1)
    a = p[:, :, 0] - lam * p[:, :, 1]
    return jnp.einsum('bhqk,bkhe->bqhe', a.astype(v.dtype), v)


def diff_mixer(q, k, v, qc, kc, vc, cos, sin, q_gain, k_gain, lam_vecs, subln, layer, ctx_out):
    heads = lambda t: t.reshape(*t.shape[:-1], DF_HEADS, 2, DF_HEAD_DIM)
    vheads = lambda t: t.reshape(*t.shape[:-1], DF_HEADS, 2 * DF_HEAD_DIM)
    q = apply_axial_rope(rms_norm(heads(q), q_gain), cos, sin)
    k = apply_axial_rope(rms_norm(heads(k), k_gain), cos, sin)
    kc = rms_norm(heads(kc), k_gain)
    v, vc = vheads(v), vheads(vc)
    lam_init = 0.8 - 0.6 * math.exp(-0.3 * layer)
    lv = lam_vecs.astype(jnp.float32)
    lam = jnp.exp(jnp.sum(lv[0] * lv[1])) - jnp.exp(jnp.sum(lv[2] * lv[3])) + lam_init
    k_all = jnp.concatenate([k, kc], axis=1)
    v_all = jnp.concatenate([v, vc], axis=1)
    b, n = q.shape[:2]
    nb = n // DF_Q_BLOCK
    qblocks = q.reshape(b, nb, DF_Q_BLOCK, DF_HEADS, 2, DF_HEAD_DIM).swapaxes(0, 1)
    y = lax.map(lambda qb: diff_attend(qb, k_all, v_all, lam), qblocks)
    y = y.swapaxes(0, 1).reshape(b, n, DF_HEADS, 2 * DF_HEAD_DIM)
    finish = lambda t: (rms_norm(t, subln) * (1.0 - lam_init)).reshape(*t.shape[:2], DF_WIDTH)
    yc = finish(diff_attend(rms_norm(heads(qc), q_gain), kc, vc, lam)) if ctx_out else None
    return finish(y), yc


def ssd_chunked(x, dt, a, bm, cm, h0, return_y):
    b, n, nh, p = x.shape
    g, ns = bm.shape[2], bm.shape[3]
    e = nh // g
    nc, q = n // SSM_CHUNK, SSM_CHUNK
    f32 = jnp.float32
    dtf = dt.astype(f32)
    xdt = (x.astype(f32) * dtf[..., None]).reshape(b, nc, q, g, e, p)
    la = (dtf * a.astype(f32)).reshape(b, nc, q, g, e).transpose(0, 3, 4, 1, 2)
    bm = bm.astype(f32).reshape(b, nc, q, g, ns)
    cm = cm.astype(f32).reshape(b, nc, q, g, ns)
    cs = jnp.cumsum(la, axis=-1)
    decay_to_end = jnp.exp(cs[..., -1:] - cs)
    states = jnp.einsum('bcjgn,bgecj,bcjgep->bcgepn', bm, decay_to_end, xdt)
    chunk_decay = jnp.exp(cs[..., -1])

    def step(s, inp):
        st, dec = inp
        return s * dec[..., None, None] + st, s

    final, prev = lax.scan(step, h0.astype(f32).reshape(b, g, e, p, ns),
                           (states.swapaxes(0, 1), jnp.moveaxis(chunk_decay, -1, 0)))
    final = final.reshape(b, nh, p, ns)
    if not return_y:
        return None, final
    prev = prev.swapaxes(0, 1)
    lower = np.tril(np.ones((q, q), dtype=bool))
    lmat = jnp.exp(jnp.where(lower, cs[..., :, None] - cs[..., None, :], -jnp.inf))
    cb = jnp.einsum('bcign,bcjgn->bgcij', cm, bm)
    y_diag = jnp.einsum('bgcij,bgecij,bcjgep->bcigep', cb, lmat, xdt)
    y_off = jnp.einsum('bcign,bcgepn,bgeci->bcigep', cm, prev, jnp.exp(cs))
    return (y_diag + y_off).reshape(b, n, nh, p).astype(x.dtype), final


def ssm_mixer(z, xbc, dt_raw, zc, xbc_c, dt_raw_c, conv_w, conv_b, dt_bias, a_log, d_skip, norm_gain, ctx_out):
    def prep(xbc_, dt_raw_):
        u = jax.nn.silu(depthwise_conv(xbc_, conv_w, conv_b))
        xs, bm, cm = jnp.split(u, [SSM_D_INNER, SSM_D_INNER + SSM_GROUPS * SSM_STATE], axis=-1)
        bsz, n = u.shape[:2]
        xs = xs.reshape(bsz, n, SSM_HEADS, SSM_HEAD_DIM)
        bm = bm.reshape(bsz, n, SSM_GROUPS, SSM_STATE)
        cm = cm.reshape(bsz, n, SSM_GROUPS, SSM_STATE)
        dt = jax.nn.softplus(dt_raw_.reshape(bsz, n, 2, SSM_HEADS) + dt_bias)
        return xs, bm, cm, dt

    def finish(y, xs, zz):
        y = y + xs * d_skip[:, None]
        return rms_norm(y.reshape(*y.shape[:2], SSM_D_INNER) * jax.nn.silu(zz), norm_gain)

    flip = lambda t: t[:, ::-1]
    a = -jnp.exp(a_log.astype(jnp.float32))
    xs, bm, cm, dt = prep(xbc, dt_raw)
    xc, bc, cc, dtc = prep(xbc_c, dt_raw_c)
    h0 = jnp.zeros((xc.shape[0], SSM_HEADS, SSM_HEAD_DIM, SSM_STATE), jnp.float32)
    yc_f, s_f = ssd_chunked(xc, dtc[:, :, 0], a[0], bc, cc, h0, ctx_out)
    yc_b, s_b = ssd_chunked(flip(xc), flip(dtc[:, :, 1]), a[1], flip(bc), flip(cc), h0, ctx_out)
    y_f, _ = ssd_chunked(xs, dt[:, :, 0], a[0], bm, cm, s_f, True)
    y_b, _ = ssd_chunked(flip(xs), flip(dt[:, :, 1]), a[1], flip(bm), flip(cm), s_b, True)
    y = finish(y_f + flip(y_b), xs, z)
    yc = finish(yc_f + flip(yc_b), xc, zc) if ctx_out else None
    return y, yc


def token_mixers(h, hc, cos, sin, layer, ctx_out, w_in, na_q_gain, na_k_gain, na_rpb, df_q_gain, df_k_gain,
                 df_lambda, df_subln, ssm_conv_w, ssm_conv_b, ssm_dt_bias, ssm_a_log, ssm_d, ssm_norm):
    cuts = [int(v) for v in np.cumsum(IN_SPLITS)[:-1]]
    na_q, na_k, na_v, df_q, df_k, df_v, s_z, s_xbc, s_dt = jnp.split(h @ w_in, cuts, axis=-1)
    na_qc, na_kc, na_vc, df_qc, df_kc, df_vc, s_zc, s_xbcc, s_dtc = jnp.split(hc @ w_in, cuts, axis=-1)
    b, n = h.shape[:2]
    nc = hc.shape[1]
    na_heads = lambda t: t.reshape(*t.shape[:-1], NA_HEADS, NA_HEAD_DIM)
    kc, vc = rms_norm(na_heads(na_kc), na_k_gain), na_heads(na_vc)
    y_na = neighbourhood_attention(rms_norm(na_heads(na_q), na_q_gain), rms_norm(na_heads(na_k), na_k_gain),
                                   na_heads(na_v), kc, vc, na_rpb).reshape(b, n, NA_WIDTH)
    y_df, yc_df = diff_mixer(df_q, df_k, df_v, df_qc, df_kc, df_vc, cos, sin, df_q_gain, df_k_gain,
                             df_lambda, df_subln, layer, ctx_out)
    y_ssm, yc_ssm = ssm_mixer(s_z, s_xbc, s_dt, s_zc, s_xbcc, s_dtc, ssm_conv_w, ssm_conv_b, ssm_dt_bias,
                              ssm_a_log, ssm_d, ssm_norm, ctx_out)
    y = jnp.concatenate([y_na, y_df, y_ssm], axis=-1)
    if not ctx_out:
        return y, None
    yc_na = dense_attention(rms_norm(na_heads(na_qc), na_q_gain), kc, vc).reshape(b, nc, NA_WIDTH)
    return y, jnp.concatenate([yc_na, yc_df, yc_ssm], axis=-1)


def conv_glu(h, w_up, conv_w, conv_b, w_down):
    gate, val = jnp.split(h @ w_up, 2, axis=-1)
    return (jax.nn.silu(depthwise_conv(gate, conv_w, conv_b)) * val) @ w_down


def setup_inputs(seed: int = 0) -> dict:
    key = jax.random.key(seed)
    ks = iter(jax.random.split(key, 40))
    nrm = lambda shape, scale: jax.random.normal(next(ks), shape, jnp.float32) * scale
    gain = lambda shape: 1.0 + nrm(shape, 0.02)
    dt0 = jnp.exp(jax.random.uniform(next(ks), (DEPTH, 2, SSM_HEADS), jnp.float32,
                                     minval=math.log(1e-3), maxval=math.log(1e-1)))
    a0 = jax.random.uniform(next(ks), (DEPTH, 2, SSM_HEADS), jnp.float32, minval=1.0, maxval=16.0)
    return {
        'x': nrm((BATCH, SEQ, D_MODEL), 1.0),
        'c': nrm((BATCH, D_MODEL), 1.0),
        'ctx': nrm((BATCH, CTX_LEN, D_MODEL), 1.0),
        'c_ctx': nrm((D_MODEL,), 1.0),
        'w_ada': nrm((DEPTH, D_MODEL, 6 * D_MODEL), 0.5 * D_MODEL ** -0.5),
        'b_ada': nrm((DEPTH, 6 * D_MODEL), 0.02),
        'g_mix': gain((DEPTH, D_MODEL)),
        'g_ffn': gain((DEPTH, D_MODEL)),
        'w_in': nrm((DEPTH, D_MODEL, IN_COLS), D_MODEL ** -0.5),
        'na_q_gain': gain((DEPTH, NA_HEAD_DIM)),
        'na_k_gain': gain((DEPTH, NA_HEAD_DIM)),
        'na_rpb': nrm((DEPTH, NA_HEADS, 2 * NA_ROWS - 1, 2 * NA_COLS - 1), 0.02),
        'df_q_gain': gain((DEPTH, DF_HEAD_DIM)),
        'df_k_gain': gain((DEPTH, DF_HEAD_DIM)),
        'df_lambda': nrm((DEPTH, 4, DF_HEAD_DIM), 0.1),
        'df_subln': gain((DEPTH, 2 * DF_HEAD_DIM)),
        'ssm_conv_w': nrm((DEPTH, SSM_CONV, SSM_CONV_DIM), SSM_CONV ** -0.5),
        'ssm_conv_b': nrm((DEPTH, SSM_CONV_DIM), 0.02),
        'ssm_dt_bias': dt0 + jnp.log(-jnp.expm1(-dt0)),
        'ssm_a_log': jnp.log(a0),
        'ssm_d': gain((DEPTH, SSM_HEADS)),
        'ssm_norm': gain((DEPTH, SSM_D_INNER)),
        'w_out': nrm((DEPTH, D_MIX, D_MODEL), D_MIX ** -0.5),
        'ffn_w_up': nrm((DEPTH, D_MODEL, 2 * D_FF), D_MODEL ** -0.5),
        'ffn_conv_w': nrm((DEPTH, FFN_CONV, D_FF), FFN_CONV ** -0.5),
        'ffn_conv_b': nrm((DEPTH, D_FF), 0.02),
        'ffn_w_down': nrm((DEPTH, D_FF, D_MODEL), D_FF ** -0.5),
    }


def reference(x, c, ctx, c_ctx, w_ada, b_ada, g_mix, g_ffn, w_in, na_q_gain, na_k_gain, na_rpb,
              df_q_gain, df_k_gain, df_lambda, df_subln, ssm_conv_w, ssm_conv_b, ssm_dt_bias, ssm_a_log,
              ssm_d, ssm_norm, w_out, ffn_w_up, ffn_conv_w, ffn_conv_b, ffn_w_down):
    cos, sin = axial_rope_tables(x.shape[1], DF_HEAD_DIM)
    for l in range(DEPTH):
        ctx_out = l < DEPTH - 1
        sh1, sc1, g1, sh2, sc2, g2 = [m[:, None, :] for m in
                                      jnp.split(jax.nn.silu(c) @ w_ada[l] + b_ada[l], 6, axis=-1)]
        csh1, csc1, cg1, csh2, csc2, cg2 = jnp.split(jax.nn.silu(c_ctx) @ w_ada[l] + b_ada[l], 6, axis=-1)
        h = rms_norm(x, g_mix[l]) * (1.0 + sc1) + sh1
        hc = rms_norm(ctx, g_mix[l]) * (1.0 + csc1) + csh1
        y, yc = token_mixers(h, hc, cos, sin, l, ctx_out, w_in[l], na_q_gain[l], na_k_gain[l], na_rpb[l],
                             df_q_gain[l], df_k_gain[l], df_lambda[l], df_subln[l], ssm_conv_w[l], ssm_conv_b[l],
                             ssm_dt_bias[l], ssm_a_log[l], ssm_d[l], ssm_norm[l])
        x = x + g1 * (y @ w_out[l])
        h = rms_norm(x, g_ffn[l]) * (1.0 + sc2) + sh2
        x = x + g2 * conv_glu(h, ffn_w_up[l], ffn_conv_w[l], ffn_conv_b[l], ffn_w_down[l])
        if ctx_out:
            ctx = ctx + cg1 * (yc @ w_out[l])
            hc = rms_norm(ctx, g_ffn[l]) * (1.0 + csc2) + csh2
            ctx = ctx + cg2 * conv_glu(hc, ffn_w_up[l], ffn_conv_w[l], ffn_conv_b[l], ffn_w_down[l])
    return x
```

```python
import functools
import math

import numpy as np
import jax
import jax.numpy as jnp
from jax import lax
from jax.experimental import pallas as pl
from jax.experimental.pallas import tpu as pltpu

F32 = jnp.float32
BF16 = jnp.bfloat16

D_MODEL = 1024
DEPTH = 2
GRID_W = 64
NORM_EPS = 1e-6
NA_HEADS = 4
NA_HEAD_DIM = 64
NA_ROWS = 8
NA_COLS = 16
NA_WIDTH = NA_HEADS * NA_HEAD_DIM
DF_HEADS = 4
DF_HEAD_DIM = 32
DF_WIDTH = DF_HEADS * 2 * DF_HEAD_DIM
ROPE_BASE = 10000.0
SSM_HEADS = 8
SSM_HEAD_DIM = 64
SSM_D_INNER = SSM_HEADS * SSM_HEAD_DIM
SSM_GROUPS = 2
SSM_STATE = 128
SSM_CONV = 5
SSM_CHUNK = 128
SSM_CONV_DIM = SSM_D_INNER + 2 * SSM_GROUPS * SSM_STATE
D_MIX = NA_WIDTH + DF_WIDTH + SSM_D_INNER
IN_COLS = 3 * NA_WIDTH + 3 * DF_WIDTH + SSM_D_INNER + SSM_CONV_DIM + 2 * SSM_HEADS
D_FF = 2816
FFN_CONV = 3

LANE = 128
SUBLANE = 8
IN_COLS_PAD = -(-IN_COLS // LANE) * LANE
DT_BLOCK = IN_COLS_PAD // LANE - 1
VMEM_LIMIT = 56 * 1024 * 1024

IN_TM = 512
OUT_TM = 512
FFN_TM = 512
FFN_CHUNK = D_FF // 2
DF_TQ = 128


def _cparams(sem):
    return pltpu.CompilerParams(dimension_semantics=sem, vmem_limit_bytes=VMEM_LIMIT)


def _dotf(a, b):
    return jnp.dot(a, b, preferred_element_type=F32)


def _dot_nt(a, b):
    return lax.dot_general(a, b, (((1,), (1,)), ((), ())), preferred_element_type=F32)


def _split2(x):
    hi = x.astype(BF16)
    lo = (x - hi.astype(F32)).astype(BF16)
    return hi, lo


def _split3(x):
    hi = x.astype(BF16)
    r = x - hi.astype(F32)
    mid = r.astype(BF16)
    lo = (r - mid.astype(F32)).astype(BF16)
    return hi, mid, lo


def _segmean(xsq, bd):
    hi, lo = _split2(xsq)
    return _dotf(hi, bd) + _dotf(lo, bd)


def _sigmoid(x):
    return 1.0 / (1.0 + jnp.exp(-x))


def _silu(x):
    return x * _sigmoid(x)


def _rms_rows(xf, gain):
    ms = jnp.mean(xf * xf, axis=-1, keepdims=True)
    return xf * lax.rsqrt(ms + NORM_EPS) * gain


def _ada_kernel(c_ref, w_ref, b_ref, o_ref):
    s = _silu(c_ref[...])
    shi, slo = _split2(s)
    whi, wlo = _split2(w_ref[0])
    o_ref[0] = _dotf(shi, whi) + _dotf(shi, wlo) + _dotf(slo, whi) + b_ref[0]


def _ada(cc, w_ada, b_ada):
    rows = cc.shape[0]
    nblk = 6
    return pl.pallas_call(
        _ada_kernel,
        grid=(DEPTH, nblk),
        in_specs=[
            pl.BlockSpec((rows, D_MODEL), lambda l, j: (0, 0)),
            pl.BlockSpec((1, D_MODEL, D_MODEL), lambda l, j: (l, 0, j)),
            pl.BlockSpec((1, 1, D_MODEL), lambda l, j: (l, 0, j)),
        ],
        out_specs=pl.BlockSpec((1, rows, D_MODEL), lambda l, j: (l, 0, j)),
        out_shape=jax.ShapeDtypeStruct((DEPTH, rows, 6 * D_MODEL), F32),
        compiler_params=_cparams(("parallel", "parallel")),
        name="adaln",
    )(cc, w_ada, b_ada.reshape(DEPTH, 1, 6 * D_MODEL))


def _mod_spec(chunk, fixed_row):
    if fixed_row is None:
        return pl.BlockSpec((1, 1, D_MODEL), lambda b, i: (b, 0, chunk))
    return pl.BlockSpec((1, 1, D_MODEL), lambda b, i: (fixed_row, 0, chunk))


def _inproj_kernel(x_ref, sh_ref, sc_ref, g_ref, w_ref, o_ref):
    h = _rms_rows(x_ref[0], g_ref[...]) * (1.0 + sc_ref[0]) + sh_ref[0]
    o_ref[0] = _dotf(h.astype(BF16), w_ref[...]).astype(BF16)


def _inproj(x, mod, gain, w, fixed_row):
    bsz, n, _ = x.shape
    tm = min(IN_TM, n)
    return pl.pallas_call(
        _inproj_kernel,
        grid=(bsz, n // tm),
        in_specs=[
            pl.BlockSpec((1, tm, D_MODEL), lambda b, i: (b, i, 0)),
            _mod_spec(0, fixed_row),
            _mod_spec(1, fixed_row),
            pl.BlockSpec((1, D_MODEL), lambda b, i: (0, 0)),
            pl.BlockSpec((D_MODEL, IN_COLS_PAD), lambda b, i: (0, 0)),
        ],
        out_specs=pl.BlockSpec((1, tm, IN_COLS_PAD), lambda b, i: (b, i, 0)),
        out_shape=jax.ShapeDtypeStruct((bsz, n, IN_COLS_PAD), BF16),
        compiler_params=_cparams(("parallel", "parallel")),
        name="inproj",
    )(x, mod, mod, gain, w)


def _na_kernel(*refs, rows, ctx_out):
    if ctx_out:
        (q_ref, k_ref, v_ref, qc_ref, kc_ref, vc_ref, qg_ref, kg_ref, bias_ref, bd_ref,
         o_ref, oc_ref, qn_s, kn_s, kcn_s, qcn_s) = refs
    else:
        (q_ref, k_ref, v_ref, kc_ref, vc_ref, qg_ref, kg_ref, bias_ref, bd_ref,
         o_ref, qn_s, kn_s, kcn_s) = refs
    bd = bd_ref[...]
    scale = NA_HEAD_DIM ** -0.5

    def norm(x, g):
        xf = x.astype(F32)
        return xf * lax.rsqrt(_segmean(xf * xf, bd) + NORM_EPS) * g

    qn_s[...] = (norm(q_ref[0], qg_ref[...]) * scale).astype(BF16)
    kn_s[...] = norm(k_ref[0], kg_ref[...]).astype(BF16)
    kcn_s[...] = norm(kc_ref[0], kg_ref[...]).astype(BF16)

    head = lax.broadcasted_iota(jnp.int32, (1, NA_WIDTH), 1) // NA_HEAD_DIM
    hmask = [head == h for h in range(NA_HEADS)]
    wr = min(NA_ROWS, rows)
    nloc = wr * GRID_W

    def stack_heads(q):
        return jnp.concatenate([jnp.where(hmask[h], q, jnp.zeros_like(q)) for h in range(NA_HEADS)], axis=0)

    def unstack_heads(o):
        out = jnp.where(hmask[0], o[0:GRID_W], 0.0)
        for h in range(1, NA_HEADS):
            out = out + jnp.where(hmask[h], o[h * GRID_W:(h + 1) * GRID_W], 0.0)
        return out

    def row_body(r, carry):
        kr0 = jnp.clip(r - wr // 2, 0, rows - wr)
        d0 = kr0 - r + NA_ROWS - 1
        q4 = stack_heads(qn_s[pl.ds(pl.multiple_of(r * GRID_W, GRID_W), GRID_W), :])
        kstart = pl.multiple_of(kr0 * GRID_W, GRID_W)
        s_loc = _dot_nt(q4, kn_s[pl.ds(kstart, nloc), :]) + bias_ref[d0]
        s_ctx = _dot_nt(q4, kcn_s[...])
        m = jnp.maximum(jnp.max(s_loc, axis=-1, keepdims=True), jnp.max(s_ctx, axis=-1, keepdims=True))
        p_loc = jnp.exp(s_loc - m)
        p_ctx = jnp.exp(s_ctx - m)
        l = jnp.sum(p_loc, axis=-1, keepdims=True) + jnp.sum(p_ctx, axis=-1, keepdims=True)
        o = _dotf(p_loc.astype(BF16), v_ref[0, pl.ds(kstart, nloc), :]) + _dotf(p_ctx.astype(BF16), vc_ref[0])
        o = o * (1.0 / l)
        o_ref[0, pl.ds(pl.multiple_of(r * GRID_W, GRID_W), GRID_W), :] = unstack_heads(o).astype(BF16)
        return carry

    lax.fori_loop(0, rows, row_body, 0)

    if ctx_out:
        qcn_s[...] = (norm(qc_ref[0], qg_ref[...]) * scale).astype(BF16)
        nblk = qcn_s.shape[0] // GRID_W

        def ctx_body(r, carry):
            rs = pl.multiple_of(r * GRID_W, GRID_W)
            q4 = stack_heads(qcn_s[pl.ds(rs, GRID_W), :])
            s = _dot_nt(q4, kcn_s[...])
            m = jnp.max(s, axis=-1, keepdims=True)
            p = jnp.exp(s - m)
            l = jnp.sum(p, axis=-1, keepdims=True)
            o = _dotf(p.astype(BF16), vc_ref[0]) * (1.0 / l)
            oc_ref[0, pl.ds(rs, GRID_W), :] = unstack_heads(o).astype(BF16)
            return carry

        lax.fori_loop(0, nblk, ctx_body, 0)


def _na_bias_table(rpb, rows):
    wr = min(NA_ROWS, rows)
    d0 = np.arange(NA_ROWS)
    j = np.arange(wr)
    dr = np.clip(d0[:, None] + j[None, :], 0, 2 * NA_ROWS - 2)
    qc = np.arange(GRID_W)
    kc = np.arange(GRID_W)
    dc = np.clip(kc[None, :] - qc[:, None], 1 - NA_COLS, NA_COLS - 1) + NA_COLS - 1
    ws = np.clip(qc - NA_COLS // 2, 0, GRID_W - NA_COLS)
    ok = (kc[None, :] >= ws[:, None]) & (kc[None, :] < ws[:, None] + NA_COLS)
    t = rpb.astype(F32)[:, dr[:, None, :, None], dc[None, :, None, :]]
    t = jnp.where(ok[None, None, :, None, :], t, -jnp.inf)
    return t.transpose(1, 0, 2, 3, 4).reshape(NA_ROWS, NA_HEADS * GRID_W, wr * GRID_W)


def _na(p, pc, q_gain, k_gain, rpb, ctx_out):
    bsz, n, _ = p.shape
    nc = pc.shape[1]
    rows = n // GRID_W
    w = NA_WIDTH
    bias = _na_bias_table(rpb, rows)
    bd = jnp.asarray(np.kron(np.eye(NA_HEADS), np.full((NA_HEAD_DIM, NA_HEAD_DIM), 1.0 / NA_HEAD_DIM)), BF16)
    qg = jnp.tile(q_gain.astype(F32), NA_HEADS).reshape(1, w)
    kg = jnp.tile(k_gain.astype(F32), NA_HEADS).reshape(1, w)
    col = lambda nrow, j: pl.BlockSpec((1, nrow, w), lambda b: (b, 0, j))
    const = lambda shape: pl.BlockSpec(shape, lambda b: (0,) * len(shape))
    in_specs = [col(n, 0), col(n, 1), col(n, 2)]
    args = [p, p, p]
    if ctx_out:
        in_specs.append(col(nc, 0))
        args.append(pc)
    in_specs += [col(nc, 1), col(nc, 2), const((1, w)), const((1, w)), const(bias.shape), const((w, w))]
    args += [pc, pc, qg, kg, bias, bd]
    out_specs = [pl.BlockSpec((1, n, w), lambda b: (b, 0, 0))]
    out_shape = [jax.ShapeDtypeStruct((bsz, n, w), BF16)]
    scratch = [pltpu.VMEM((n, w), BF16), pltpu.VMEM((n, w), BF16), pltpu.VMEM((nc, w), BF16)]
    if ctx_out:
        out_specs.append(pl.BlockSpec((1, nc, w), lambda b: (b, 0, 0)))
        out_shape.append(jax.ShapeDtypeStruct((bsz, nc, w), BF16))
        scratch.append(pltpu.VMEM((nc, w), BF16))
    outs = pl.pallas_call(
        functools.partial(_na_kernel, rows=rows, ctx_out=ctx_out),
        grid=(bsz,),
        in_specs=in_specs,
        out_specs=out_specs,
        out_shape=out_shape,
        scratch_shapes=scratch,
        compiler_params=_cparams(("parallel",)),
        name="na_attn",
    )(*args)
    return (outs[0], outs[1]) if ctx_out else (outs[0], None)


def _rope_tables(n):
    per_axis = DF_HEAD_DIM // 2
    inv_freq = ROPE_BASE ** (-jnp.arange(0, per_axis, 2, dtype=F32) / per_axis)
    t = jnp.arange(n, dtype=jnp.int32)
    pos = jnp.stack([t // GRID_W, t % GRID_W], axis=-1).astype(F32)
    ang = pos[:, :, None] * inv_freq
    ang = jnp.concatenate([ang, ang], axis=-1).reshape(n, DF_HEAD_DIM)
    reps = DF_WIDTH // DF_HEAD_DIM
    cos = jnp.tile(jnp.cos(ang), (1, reps))
    sin = jnp.tile(jnp.sin(ang), (1, reps))
    first = (np.arange(DF_WIDTH) % (DF_HEAD_DIM // 2)) < DF_HEAD_DIM // 4
    sin_a = jnp.where(first[None, :], -sin, 0.0)
    sin_b = jnp.where(first[None, :], 0.0, sin)
    return cos, sin_a, sin_b


def _df_kernel(*refs, ctx_out, lam_init, tq):
    if ctx_out:
        (q_ref, k_ref, v_ref, qc_ref, kc_ref, vc_ref, cos_ref, sa_ref, sb_ref, qg_ref, kg_ref, sub_ref,
         lam_ref, bd32_ref, bd64_ref, o_ref, oc_ref, qn_s, kall_s, vall_s, qcn_s) = refs
    else:
        (q_ref, k_ref, v_ref, kc_ref, vc_ref, cos_ref, sa_ref, sb_ref, qg_ref, kg_ref, sub_ref,
         lam_ref, bd32_ref, bd64_ref, o_ref, qn_s, kall_s, vall_s) = refs
    n = q_ref.shape[1]
    nc = kc_ref.shape[1]
    bd32 = bd32_ref[...]
    bd64 = bd64_ref[...]
    scale = DF_HEAD_DIM ** -0.5
    lam = lam_ref[0, 0]
    shift = DF_HEAD_DIM // 4

    def norm(x, g):
        xf = x.astype(F32)
        return xf * lax.rsqrt(_segmean(xf * xf, bd32) + NORM_EPS) * g

    def rope(xn):
        return (xn * cos_ref[...] + pltpu.roll(xn, DF_WIDTH - shift, 1) * sa_ref[...]
                + pltpu.roll(xn, shift, 1) * sb_ref[...])

    qn_s[...] = (rope(norm(q_ref[0], qg_ref[...])) * scale).astype(BF16)
    kall_s[0:n, :] = rope(norm(k_ref[0], kg_ref[...])).astype(BF16)
    kall_s[n:n + nc, :] = norm(kc_ref[0], kg_ref[...]).astype(BF16)
    vall_s[0:n, :] = v_ref[0]
    vall_s[n:n + nc, :] = vc_ref[0]

    lane = lax.broadcasted_iota(jnp.int32, (1, DF_WIDTH), 1)
    comp = lane // DF_HEAD_DIM
    head = lane // (2 * DF_HEAD_DIM)
    sub_gain = sub_ref[...] * (1.0 - lam_init)

    def attend(q, kk, vv):
        t = q.shape[0]
        y = jnp.zeros((t, DF_WIDTH), F32)
        zero = jnp.zeros_like(q)
        for h in range(DF_HEADS):
            q2 = jnp.concatenate([jnp.where(comp == 2 * h, q, zero), jnp.where(comp == 2 * h + 1, q, zero)], axis=0)
            s = _dot_nt(q2, kk)
            m = jnp.max(s, axis=-1, keepdims=True)
            p = jnp.exp(s - m)
            l = jnp.sum(p, axis=-1, keepdims=True)
            o = _dotf(p.astype(BF16), vv) * (1.0 / l)
            y = y + jnp.where(head == h, o[0:t] - lam * o[t:2 * t], 0.0)
        return y * lax.rsqrt(_segmean(y * y, bd64) + NORM_EPS) * sub_gain

    def q_body(i, carry):
        rs = pl.multiple_of(i * tq, tq)
        o_ref[0, pl.ds(rs, tq), :] = attend(qn_s[pl.ds(rs, tq), :], kall_s[...], vall_s[...]).astype(BF16)
        return carry

    lax.fori_loop(0, n // tq, q_body, 0)

    if ctx_out:
        qcn_s[...] = (norm(qc_ref[0], qg_ref[...]) * scale).astype(BF16)
        tc = min(tq, nc)

        def c_body(i, carry):
            rs = pl.multiple_of(i * tc, tc)
            oc_ref[0, pl.ds(rs, tc), :] = attend(qcn_s[pl.ds(rs, tc), :], kall_s[n:n + nc, :],
                                                 vall_s[n:n + nc, :]).astype(BF16)
            return carry

        lax.fori_loop(0, nc // tc, c_body, 0)


def _df(p, pc, tables, q_gain, k_gain, lam_vecs, subln, layer, ctx_out):
    bsz, n, _ = p.shape
    nc = pc.shape[1]
    w = DF_WIDTH
    cos, sin_a, sin_b = tables
    lam_init = 0.8 - 0.6 * math.exp(-0.3 * layer)
    lv = lam_vecs.astype(F32)
    lam = (jnp.exp(jnp.sum(lv[0] * lv[1])) - jnp.exp(jnp.sum(lv[2] * lv[3])) + lam_init).reshape(1, 1)
    bd32 = jnp.asarray(np.kron(np.eye(w // DF_HEAD_DIM), np.full((DF_HEAD_DIM,) * 2, 1.0 / DF_HEAD_DIM)), BF16)
    bd64 = jnp.asarray(np.kron(np.eye(DF_HEADS), np.full((2 * DF_HEAD_DIM,) * 2, 0.5 / DF_HEAD_DIM)), BF16)
    qg = jnp.tile(q_gain.astype(F32), w // DF_HEAD_DIM).reshape(1, w)
    kg = jnp.tile(k_gain.astype(F32), w // DF_HEAD_DIM).reshape(1, w)
    sub = jnp.tile(subln.astype(F32), DF_HEADS).reshape(1, w)
    col = lambda nrow, j: pl.BlockSpec((1, nrow, w), lambda b: (b, 0, j))
    const = lambda shape: pl.BlockSpec(shape, lambda b: (0,) * len(shape))
    in_specs = [col(n, 3), col(n, 4), col(n, 5)]
    args = [p, p, p]
    if ctx_out:
        in_specs.append(col(nc, 3))
        args.append(pc)
    in_specs += [col(nc, 4), col(nc, 5), const((n, w)), const((n, w)), const((n, w)),
                 const((1, w)), const((1, w)), const((1, w)),
                 pl.BlockSpec(memory_space=pltpu.SMEM), const((w, w)), const((w, w))]
    args += [pc, pc, cos, sin_a, sin_b, qg, kg, sub, lam, bd32, bd64]
    out_specs = [pl.BlockSpec((1, n, w), lambda b: (b, 0, 0))]
    out_shape = [jax.ShapeDtypeStruct((bsz, n, w), BF16)]
    scratch = [pltpu.VMEM((n, w), BF16), pltpu.VMEM((n + nc, w), BF16), pltpu.VMEM((n + nc, w), BF16)]
    if ctx_out:
        out_specs.append(pl.BlockSpec((1, nc, w), lambda b: (b, 0, 0)))
        out_shape.append(jax.ShapeDtypeStruct((bsz, nc, w), BF16))
        scratch.append(pltpu.VMEM((nc, w), BF16))
    outs = pl.pallas_call(
        functools.partial(_df_kernel, ctx_out=ctx_out, lam_init=lam_init, tq=DF_TQ),
        grid=(bsz,),
        in_specs=in_specs,
        out_specs=out_specs,
        out_shape=out_shape,
        scratch_shapes=scratch,
        compiler_params=_cparams(("parallel",)),
        name="diff_attn",
    )(*args)
    return (outs[0], outs[1]) if ctx_out else (outs[0], None)


def _ssd_kernel(*refs, ctx_out):
    if ctx_out:
        (z_ref, xbc_ref, dt_ref, zc_ref, xbcc_ref, dtc_ref, cw_ref, cb_ref, dtb_ref, alog_ref, dsk_ref, ng_ref,
         tri_ref, o_ref, oc_ref, xpad_s, u_s, g_s, dtv_s, y_s, st_s) = refs
    else:
        (z_ref, xbc_ref, dt_ref, xbcc_ref, dtc_ref, cw_ref, cb_ref, dtb_ref, alog_ref, dsk_ref, ng_ref,
         tri_ref, o_ref, xpad_s, u_s, g_s, dtv_s, y_s, st_s) = refs
        zc_ref = oc_ref = None
    n = xbc_ref.shape[1]
    nc = xbcc_ref.shape[1]
    q = SSM_CHUNK
    ncc = nc // q
    nch = (n + nc) // q
    pad = SUBLANE
    half = SSM_CONV // 2
    cdim = SSM_CONV_DIM
    lat0 = 2 * pad + nc

    zrow = jnp.zeros((pad, cdim), F32)
    xpad_s[0:pad, :] = zrow
    xpad_s[pad:pad + nc, :] = xbcc_ref[0].astype(F32)
    xpad_s[pad + nc:lat0, :] = zrow
    xpad_s[lat0:lat0 + n, :] = xbc_ref[0].astype(F32)
    xpad_s[lat0 + n:lat0 + n + pad, :] = zrow
    dtv_s[0:nc, :] = dtc_ref[0].astype(F32)
    dtv_s[nc:nc + n, :] = dt_ref[0].astype(F32)

    lane = lax.broadcasted_iota(jnp.int32, (1, LANE), 1)
    a_row = -jnp.exp(alog_ref[...])
    tri = tri_ref[...]
    win = q + 2 * pad

    def prep_body(c, carry):
        rb = pl.multiple_of(c * q + jnp.where(c >= ncc, pad, 0), SUBLANE)
        wv = xpad_s[pl.ds(rb, win), :]
        acc = cb_ref[...] + cw_ref[half:half + 1, :] * wv[pad:pad + q]
        for k in range(SSM_CONV):
            if k != half:
                acc = acc + cw_ref[k:k + 1, :] * pltpu.roll(wv, (half - k) % win, 0)[pad:pad + q]
        rows = pl.ds(pl.multiple_of(c * q, q), q)
        u_s[rows, :] = _silu(acc)
        raw = dtv_s[rows, :] + dtb_ref[...]
        dt = jnp.maximum(raw, 0.0) + jnp.log(1.0 + jnp.exp(-jnp.abs(raw)))
        dt = jnp.where(lane < 2 * SSM_HEADS, dt, 0.0)
        la = dt * a_row
        h3 = _split3(la)
        cs = _dotf(tri, h3[0]) + _dotf(tri, h3[1]) + _dotf(tri, h3[2])
        tot = cs[q - 1:q, :]
        g_s[rows, :] = jnp.where(lane < SSM_HEADS, cs, tot - cs + la)
        dtv_s[rows, :] = dt
        return carry

    lax.fori_loop(0, nch, prep_body, 0)

    ii = lax.broadcasted_iota(jnp.int32, (q, q), 0)
    jj = lax.broadcasted_iota(jnp.int32, (q, q), 1)
    hpg = SSM_HEADS // SSM_GROUPS
    gw = hpg * SSM_HEAD_DIM
    hl = lax.broadcasted_iota(jnp.int32, (1, gw), 1) // SSM_HEAD_DIM
    emask = [hl == e for e in range(hpg)]

    def chunk_update(c, d):
        rows = pl.ds(pl.multiple_of(c * q, q), q)
        g = g_s[rows, :]
        dtv = dtv_s[rows, :]
        tot = jnp.where(lane < SSM_HEADS, g[q - 1:q, :], g[0:1, :])
        eg = jnp.exp(g)
        etot = jnp.exp(tot)
        g_t = g.T
        w_t = (jnp.exp(tot - g) * dtv).T
        dt_t = dtv.T
        keep = (ii >= jj) if d == 0 else (jj >= ii)
        ys = []
        for grp in range(SSM_GROUPS):
            bm = u_s[rows, SSM_D_INNER + grp * SSM_STATE:SSM_D_INNER + (grp + 1) * SSM_STATE]
            cm = u_s[rows, SSM_D_INNER + (SSM_GROUPS + grp) * SSM_STATE:
                     SSM_D_INNER + (SSM_GROUPS + grp + 1) * SSM_STATE]
            xs = u_s[rows, grp * gw:(grp + 1) * gw].astype(BF16)
            prev = st_s[:, grp * gw:(grp + 1) * gw]
            prev_b = prev.astype(BF16)
            cb = _dot_nt(cm.astype(BF16), bm.astype(BF16))
            bm_t = bm.T
            yg = jnp.zeros((q, gw), F32)
            sg = jnp.zeros((SSM_STATE, gw), F32)
            dec = jnp.zeros((1, gw), F32)
            for e in range(hpg):
                li = d * SSM_HEADS + grp * hpg + e
                gcol = jnp.broadcast_to(g[:, li:li + 1], (q, q))
                grow = jnp.broadcast_to(g_t[li:li + 1, :], (q, q))
                lm = jnp.exp(jnp.where(keep, gcol - grow, -jnp.inf))
                mm = cb * lm * jnp.broadcast_to(dt_t[li:li + 1, :], (q, q))
                l2 = jnp.broadcast_to(eg[:, li:li + 1], (q, q)) * cm
                yh = _dotf(mm.astype(BF16), xs) + _dotf(l2.astype(BF16), prev_b)
                yg = yg + jnp.where(emask[e], yh, 0.0)
                bw = bm_t * jnp.broadcast_to(w_t[li:li + 1, :], (q, q))
                sg = sg + jnp.where(emask[e], _dotf(bw.astype(BF16), xs), 0.0)
                dec = dec + jnp.where(emask[e], jnp.broadcast_to(etot[:, li:li + 1], (1, gw)), 0.0)
            st_s[:, grp * gw:(grp + 1) * gw] = prev * dec + sg
            ys.append(yg)
        return jnp.concatenate(ys, axis=-1)

    st_s[...] = jnp.zeros_like(st_s)

    def fwd_body(c, carry):
        y_s[pl.ds(pl.multiple_of(c * q, q), q), :] = chunk_update(c, 0)
        return carry

    lax.fori_loop(0, nch, fwd_body, 0)

    st_s[...] = jnp.zeros_like(st_s)

    def finish(c, yb, zz):
        rows = pl.ds(pl.multiple_of(c * q, q), q)
        y = y_s[rows, :] + yb + u_s[rows, 0:SSM_D_INNER] * dsk_ref[...]
        y = y * _silu(zz.astype(F32))
        return _rms_rows(y, ng_ref[...]).astype(BF16)

    def bwd_ctx_body(s, carry):
        c = ncc - 1 - s
        yb = chunk_update(c, 1)
        if ctx_out:
            rows = pl.ds(pl.multiple_of(c * q, q), q)
            oc_ref[0, rows, :] = finish(c, yb, zc_ref[0, rows, :])
        return carry

    lax.fori_loop(0, ncc, bwd_ctx_body, 0)

    def bwd_lat_body(s, carry):
        c = nch - 1 - s
        yb = chunk_update(c, 1)
        rows = pl.ds(pl.multiple_of((c - ncc) * q, q), q)
        o_ref[0, rows, :] = finish(c, yb, z_ref[0, rows, :])
        return carry

    lax.fori_loop(0, nch - ncc, bwd_lat_body, 0)


def _ssd(p, pc, conv_w, conv_b, dt_bias, a_log, d_skip, norm_gain, ctx_out):
    bsz, n, _ = p.shape
    nc = pc.shape[1]
    di = SSM_D_INNER
    q = SSM_CHUNK
    padl = lambda v: jnp.pad(v.astype(F32).reshape(1, -1), ((0, 0), (0, LANE - 2 * SSM_HEADS)))
    dtb = padl(dt_bias)
    alog = padl(a_log)
    dsk = jnp.repeat(d_skip.astype(F32), SSM_HEAD_DIM).reshape(1, di)
    ng = norm_gain.astype(F32).reshape(1, di)
    tri = jnp.asarray(np.tril(np.ones((q, q))), BF16)
    z_blk = (3 * NA_WIDTH + 3 * DF_WIDTH) // di
    x_blk = (3 * NA_WIDTH + 3 * DF_WIDTH + di) // SSM_CONV_DIM
    spec = lambda nrow, w, j: pl.BlockSpec((1, nrow, w), lambda b: (b, 0, j))
    const = lambda shape: pl.BlockSpec(shape, lambda b: (0,) * len(shape))
    in_specs = [spec(n, di, z_blk), spec(n, SSM_CONV_DIM, x_blk), spec(n, LANE, DT_BLOCK)]
    args = [p, p, p]
    if ctx_out:
        in_specs.append(spec(nc, di, z_blk))
        args.append(pc)
    in_specs += [spec(nc, SSM_CONV_DIM, x_blk), spec(nc, LANE, DT_BLOCK),
                 const((SSM_CONV, SSM_CONV_DIM)), const((1, SSM_CONV_DIM)), const((1, LANE)), const((1, LANE)),
                 const((1, di)), const((1, di)), const((q, q))]
    args += [pc, pc, conv_w.astype(F32), conv_b.astype(F32).reshape(1, -1), dtb, alog, dsk, ng, tri]
    out_specs = [pl.BlockSpec((1, n, di), lambda b: (b, 0, 0))]
    out_shape = [jax.ShapeDtypeStruct((bsz, n, di), BF16)]
    if ctx_out:
        out_specs.append(pl.BlockSpec((1, nc, di), lambda b: (b, 0, 0)))
        out_shape.append(jax.ShapeDtypeStruct((bsz, nc, di), BF16))
    tot = n + nc
    scratch = [pltpu.VMEM((tot + 3 * SUBLANE, SSM_CONV_DIM), F32), pltpu.VMEM((tot, SSM_CONV_DIM), F32),
               pltpu.VMEM((tot, LANE), F32), pltpu.VMEM((tot, LANE), F32), pltpu.VMEM((tot, di), F32),
               pltpu.VMEM((SSM_STATE, di), F32)]
    outs = pl.pallas_call(
        functools.partial(_ssd_kernel, ctx_out=ctx_out),
        grid=(bsz,),
        in_specs=in_specs,
        out_specs=out_specs,
        out_shape=out_shape,
        scratch_shapes=scratch,
        compiler_params=_cparams(("parallel",)),
        name="ssd",
    )(*args)
    return (outs[0], outs[1]) if ctx_out else (outs[0], None)


def _outproj_kernel(x_ref, a_ref, b_ref, c_ref, g_ref, w_ref, o_ref):
    y = jnp.concatenate([a_ref[0], b_ref[0], c_ref[0]], axis=-1)
    o_ref[0] = x_ref[0] + g_ref[0] * _dotf(y, w_ref[...])


def _outproj(x, y_na, y_df, y_ssm, mod, w, fixed_row):
    bsz, n, _ = x.shape
    tm = min(OUT_TM, n)
    row = lambda width: pl.BlockSpec((1, tm, width), lambda b, i: (b, i, 0))
    return pl.pallas_call(
        _outproj_kernel,
        grid=(bsz, n // tm),
        in_specs=[row(D_MODEL), row(NA_WIDTH), row(DF_WIDTH), row(SSM_D_INNER), _mod_spec(2, fixed_row),
                  pl.BlockSpec((D_MIX, D_MODEL), lambda b, i: (0, 0))],
        out_specs=row(D_MODEL),
        out_shape=jax.ShapeDtypeStruct((bsz, n, D_MODEL), F32),
        compiler_params=_cparams(("parallel", "parallel")),
        name="outproj",
    )(x, y_na, y_df, y_ssm, mod, w)


def _ffn_kernel(x_ref, xp_ref, xn_ref, sh_ref, sc_ref, g2_ref, gn_ref, wu_ref, cw_ref, cb_ref, wd_ref, o_ref,
                *, n_tiles):
    i = pl.program_id(1)
    tm = x_ref.shape[1]
    halo = xp_ref.shape[1]

    def modnorm(xf):
        return _rms_rows(xf, gn_ref[...]) * (1.0 + sc_ref[0]) + sh_ref[0]

    xm = x_ref[0]
    hm = modnorm(xm)
    hp = modnorm(xp_ref[0]) * jnp.where(i > 0, 1.0, 0.0)
    hn = modnorm(xn_ref[0]) * jnp.where(i < n_tiles - 1, 1.0, 0.0)
    h_ext = jnp.concatenate([hp, hm, hn], axis=0).astype(BF16)
    h_mid = hm.astype(BF16)
    ext = tm + 2 * halo
    acc = jnp.zeros((tm, D_MODEL), F32)
    for c in range(D_FF // FFN_CHUNK):
        lo = c * FFN_CHUNK
        gate = _dotf(h_ext, wu_ref[:, lo:lo + FFN_CHUNK])
        val = _dotf(h_mid, wu_ref[:, D_FF + lo:D_FF + lo + FFN_CHUNK])
        conv = (cb_ref[:, lo:lo + FFN_CHUNK]
                + cw_ref[0:1, lo:lo + FFN_CHUNK] * pltpu.roll(gate, 1, 0)[halo:halo + tm]
                + cw_ref[1:2, lo:lo + FFN_CHUNK] * gate[halo:halo + tm]
                + cw_ref[2:3, lo:lo + FFN_CHUNK] * pltpu.roll(gate, ext - 1, 0)[halo:halo + tm])
        act = (_silu(conv) * val).astype(BF16)
        acc = acc + _dotf(act, wd_ref[lo:lo + FFN_CHUNK, :])
    o_ref[0] = xm + g2_ref[0] * acc


def _ffn(x, mod, gain, w_up, conv_w, conv_b, w_down, fixed_row):
    bsz, n, _ = x.shape
    tm = min(FFN_TM, n)
    n_tiles = n // tm
    halo = SUBLANE
    hb = tm // halo
    last = n // halo - 1
    const = lambda shape: pl.BlockSpec(shape, lambda b, i: (0,) * len(shape), pipeline_mode=pl.Buffered(1))
    return pl.pallas_call(
        functools.partial(_ffn_kernel, n_tiles=n_tiles),
        grid=(bsz, n_tiles),
        in_specs=[
            pl.BlockSpec((1, tm, D_MODEL), lambda b, i: (b, i, 0)),
            pl.BlockSpec((1, halo, D_MODEL), lambda b, i: (b, jnp.maximum(i * hb - 1, 0), 0)),
            pl.BlockSpec((1, halo, D_MODEL), lambda b, i: (b, jnp.minimum((i + 1) * hb, last), 0)),
            _mod_spec(3, fixed_row),
            _mod_spec(4, fixed_row),
            _mod_spec(5, fixed_row),
            pl.BlockSpec((1, D_MODEL), lambda b, i: (0, 0)),
            const((D_MODEL, 2 * D_FF)),
            pl.BlockSpec((FFN_CONV, D_FF), lambda b, i: (0, 0)),
            pl.BlockSpec((1, D_FF), lambda b, i: (0, 0)),
            const((D_FF, D_MODEL)),
        ],
        out_specs=pl.BlockSpec((1, tm, D_MODEL), lambda b, i: (b, i, 0)),
        out_shape=jax.ShapeDtypeStruct((bsz, n, D_MODEL), F32),
        compiler_params=_cparams(("parallel", "parallel")),
        name="ffn",
    )(x, x, x, mod, mod, mod, gain, w_up, conv_w, conv_b, w_down)


def kernel(x, c, ctx, c_ctx, w_ada, b_ada, g_mix, g_ffn, w_in, na_q_gain, na_k_gain, na_rpb, df_q_gain, df_k_gain, df_lambda, df_subln, ssm_conv_w, ssm_conv_b, ssm_dt_bias, ssm_a_log, ssm_d, ssm_norm, w_out, ffn_w_up, ffn_conv_w, ffn_conv_b, ffn_w_down):
    bsz, n, _ = x.shape
    ctx_row = bsz
    mod_rows = -(-(bsz + 1) // SUBLANE) * SUBLANE
    cc = jnp.concatenate([c, c_ctx[None, :], jnp.zeros((mod_rows - bsz - 1, D_MODEL), c.dtype)], axis=0)
    mods = _ada(cc.astype(F32), w_ada.astype(F32), b_ada.astype(F32))
    mods = mods.reshape(DEPTH, mod_rows, 1, 6 * D_MODEL)
    tables = _rope_tables(n)
    w_in_b = jnp.pad(w_in, ((0, 0), (0, 0), (0, IN_COLS_PAD - IN_COLS))).astype(BF16)
    w_out_b = w_out.astype(BF16)
    w_up_b = ffn_w_up.astype(BF16)
    w_down_b = ffn_w_down.astype(BF16)
    x = x.astype(F32)
    ctx = ctx.astype(F32)
    for l in range(DEPTH):
        ctx_out = l < DEPTH - 1
        mod = mods[l]
        gm = g_mix[l].astype(F32).reshape(1, D_MODEL)
        gf = g_ffn[l].astype(F32).reshape(1, D_MODEL)
        p = _inproj(x, mod, gm, w_in_b[l], None)
        pc = _inproj(ctx, mod, gm, w_in_b[l], ctx_row)
        y_na, yc_na = _na(p, pc, na_q_gain[l], na_k_gain[l], na_rpb[l], ctx_out)
        y_df, yc_df = _df(p, pc, tables, df_q_gain[l], df_k_gain[l], df_lambda[l], df_subln[l], l, ctx_out)
        y_ssm, yc_ssm = _ssd(p, pc, ssm_conv_w[l], ssm_conv_b[l], ssm_dt_bias[l], ssm_a_log[l], ssm_d[l],
                             ssm_norm[l], ctx_out)
        cw = ffn_conv_w[l].astype(F32)
        cb = ffn_conv_b[l].astype(F32).reshape(1, D_FF)
        x = _outproj(x, y_na, y_df, y_ssm, mod, w_out_b[l], None)
        x = _ffn(x, mod, gf, w_up_b[l], cw, cb, w_down_b[l], None)
        if ctx_out:
            ctx = _outproj(ctx, yc_na, yc_df, yc_ssm, mod, w_out_b[l], ctx_row)
            ctx = _ffn(ctx, mod, gf, w_up_b[l], cw, cb, w_down_b[l], ctx_row)
    return x
```

```python
import functools
import math

import numpy as np
import jax
import jax.numpy as jnp
from jax import lax
from jax.experimental import pallas as pl
from jax.experimental.pallas import tpu as pltpu

F32 = jnp.float32
BF16 = jnp.bfloat16

D_MODEL = 1024
DEPTH = 2
GRID_W = 64
NORM_EPS = 1e-6
NA_HEADS = 4
NA_HEAD_DIM = 64
NA_ROWS = 8
NA_COLS = 16
NA_WIDTH = NA_HEADS * NA_HEAD_DIM
DF_HEADS = 4
DF_HEAD_DIM = 32
DF_WIDTH = DF_HEADS * 2 * DF_HEAD_DIM
ROPE_BASE = 10000.0
SSM_HEADS = 8
SSM_HEAD_DIM = 64
SSM_D_INNER = SSM_HEADS * SSM_HEAD_DIM
SSM_GROUPS = 2
SSM_STATE = 128
SSM_CONV = 5
SSM_CHUNK = 128
SSM_CONV_DIM = SSM_D_INNER + 2 * SSM_GROUPS * SSM_STATE
D_MIX = NA_WIDTH + DF_WIDTH + SSM_D_INNER
IN_COLS = 3 * NA_WIDTH + 3 * DF_WIDTH + SSM_D_INNER + SSM_CONV_DIM + 2 * SSM_HEADS
D_FF = 2816
FFN_CONV = 3

LANE = 128
SUBLANE = 8
IN_COLS_PAD = -(-IN_COLS // LANE) * LANE
DT_BLOCK = IN_COLS_PAD // LANE - 1
VMEM_LIMIT = 56 * 1024 * 1024

IN_TM = 512
OUT_TM = 512
FFN_TM = 512
FFN_CHUNK = D_FF // 2
DF_TQ = 128
CONV_STRIP = 2 * LANE
NA_UNROLL = 4
LOG2E = math.log2(math.e)


def _cparams(sem):
    return pltpu.CompilerParams(dimension_semantics=sem, vmem_limit_bytes=VMEM_LIMIT)


def _dotf(a, b):
    return jnp.dot(a, b, preferred_element_type=F32)


def _dot_nt(a, b):
    return lax.dot_general(a, b, (((1,), (1,)), ((), ())), preferred_element_type=F32)


def _split2(x):
    hi = x.astype(BF16)
    lo = (x - hi.astype(F32)).astype(BF16)
    return hi, lo


def _split3(x):
    hi = x.astype(BF16)
    r = x - hi.astype(F32)
    mid = r.astype(BF16)
    lo = (r - mid.astype(F32)).astype(BF16)
    return hi, mid, lo


def _segmean(xsq, bd):
    hi, lo = _split2(xsq)
    return _dotf(hi, bd) + _dotf(lo, bd)


def _sigmoid(x):
    return 1.0 / (1.0 + jnp.exp(-x))


def _silu(x):
    return x * _sigmoid(x)


def _rms_rows(xf, gain):
    ms = jnp.mean(xf * xf, axis=-1, keepdims=True)
    return xf * lax.rsqrt(ms + NORM_EPS) * gain


def _ada_kernel(c_ref, w_ref, b_ref, o_ref):
    s = _silu(c_ref[...])
    shi, slo = _split2(s)
    whi, wlo = _split2(w_ref[0])
    o_ref[0] = _dotf(shi, whi) + _dotf(shi, wlo) + _dotf(slo, whi) + b_ref[0]


def _ada(cc, w_ada, b_ada):
    rows = cc.shape[0]
    nblk = 6
    return pl.pallas_call(
        _ada_kernel,
        grid=(DEPTH, nblk),
        in_specs=[
            pl.BlockSpec((rows, D_MODEL), lambda l, j: (0, 0)),
            pl.BlockSpec((1, D_MODEL, D_MODEL), lambda l, j: (l, 0, j)),
            pl.BlockSpec((1, 1, D_MODEL), lambda l, j: (l, 0, j)),
        ],
        out_specs=pl.BlockSpec((1, rows, D_MODEL), lambda l, j: (l, 0, j)),
        out_shape=jax.ShapeDtypeStruct((DEPTH, rows, 6 * D_MODEL), F32),
        compiler_params=_cparams(("parallel", "parallel")),
        name="adaln",
    )(cc, w_ada, b_ada.reshape(DEPTH, 1, 6 * D_MODEL))


def _mod_spec(chunk, fixed_row):
    if fixed_row is None:
        return pl.BlockSpec((1, 1, D_MODEL), lambda b, i: (b, 0, chunk))
    return pl.BlockSpec((1, 1, D_MODEL), lambda b, i: (fixed_row, 0, chunk))


def _inproj_kernel(x_ref, sh_ref, sc_ref, g_ref, w_ref, o_ref):
    h = _rms_rows(x_ref[0], g_ref[...]) * (1.0 + sc_ref[0]) + sh_ref[0]
    o_ref[0] = _dotf(h.astype(BF16), w_ref[...]).astype(BF16)


def _inproj(x, mod, gain, w, fixed_row):
    bsz, n, _ = x.shape
    tm = min(IN_TM, n)
    return pl.pallas_call(
        _inproj_kernel,
        grid=(bsz, n // tm),
        in_specs=[
            pl.BlockSpec((1, tm, D_MODEL), lambda b, i: (b, i, 0)),
            _mod_spec(0, fixed_row),
            _mod_spec(1, fixed_row),
            pl.BlockSpec((1, D_MODEL), lambda b, i: (0, 0)),
            pl.BlockSpec((D_MODEL, IN_COLS_PAD), lambda b, i: (0, 0)),
        ],
        out_specs=pl.BlockSpec((1, tm, IN_COLS_PAD), lambda b, i: (b, i, 0)),
        out_shape=jax.ShapeDtypeStruct((bsz, n, IN_COLS_PAD), BF16),
        compiler_params=_cparams(("parallel", "parallel")),
        name="inproj",
    )(x, mod, mod, gain, w)


def _na_kernel(*refs, rows, ctx_out):
    if ctx_out:
        (q_ref, k_ref, v_ref, qc_ref, kc_ref, vc_ref, qg_ref, kg_ref, bias_ref, bd_ref,
         o_ref, oc_ref, qn_s, kn_s, kcn_s, qcn_s) = refs
    else:
        (q_ref, k_ref, v_ref, kc_ref, vc_ref, qg_ref, kg_ref, bias_ref, bd_ref,
         o_ref, qn_s, kn_s, kcn_s) = refs
    bd = bd_ref[...]
    scale = NA_HEAD_DIM ** -0.5 * LOG2E

    def norm(x, g):
        xf = x.astype(F32)
        return xf * lax.rsqrt(_segmean(xf * xf, bd) + NORM_EPS) * g

    qn_s[...] = (norm(q_ref[0], qg_ref[...]) * scale).astype(BF16)
    kn_s[...] = norm(k_ref[0], kg_ref[...]).astype(BF16)
    kcn_s[...] = norm(kc_ref[0], kg_ref[...]).astype(BF16)

    head = lax.broadcasted_iota(jnp.int32, (1, NA_WIDTH), 1) // NA_HEAD_DIM
    hmask = [head == h for h in range(NA_HEADS)]
    wr = min(NA_ROWS, rows)
    nloc = wr * GRID_W

    def stack_heads(q):
        return jnp.concatenate([jnp.where(hmask[h], q, jnp.zeros_like(q)) for h in range(NA_HEADS)], axis=0)

    def unstack_heads(o):
        out = jnp.where(hmask[0], o[0:GRID_W], 0.0)
        for h in range(1, NA_HEADS):
            out = out + jnp.where(hmask[h], o[h * GRID_W:(h + 1) * GRID_W], 0.0)
        return out

    def row_body(r, carry):
        kr0 = jnp.clip(r - wr // 2, 0, rows - wr)
        d0 = kr0 - r + NA_ROWS - 1
        q4 = stack_heads(qn_s[pl.ds(pl.multiple_of(r * GRID_W, GRID_W), GRID_W), :])
        kstart = pl.multiple_of(kr0 * GRID_W, GRID_W)
        s_loc = _dot_nt(q4, kn_s[pl.ds(kstart, nloc), :]) + bias_ref[d0]
        s_ctx = _dot_nt(q4, kcn_s[...])
        m = jnp.maximum(jnp.max(s_loc, axis=-1, keepdims=True), jnp.max(s_ctx, axis=-1, keepdims=True))
        p_loc = jnp.exp2(s_loc - m)
        p_ctx = jnp.exp2(s_ctx - m)
        l = jnp.sum(p_loc, axis=-1, keepdims=True) + jnp.sum(p_ctx, axis=-1, keepdims=True)
        o = _dotf(p_loc.astype(BF16), v_ref[0, pl.ds(kstart, nloc), :]) + _dotf(p_ctx.astype(BF16), vc_ref[0])
        o = o * (1.0 / l)
        o_ref[0, pl.ds(pl.multiple_of(r * GRID_W, GRID_W), GRID_W), :] = unstack_heads(o).astype(BF16)
        return carry

    lax.fori_loop(0, rows, row_body, 0, unroll=NA_UNROLL)

    if ctx_out:
        qcn_s[...] = (norm(qc_ref[0], qg_ref[...]) * scale).astype(BF16)
        nblk = qcn_s.shape[0] // GRID_W

        def ctx_body(r, carry):
            rs = pl.multiple_of(r * GRID_W, GRID_W)
            q4 = stack_heads(qcn_s[pl.ds(rs, GRID_W), :])
            s = _dot_nt(q4, kcn_s[...])
            m = jnp.max(s, axis=-1, keepdims=True)
            p = jnp.exp2(s - m)
            l = jnp.sum(p, axis=-1, keepdims=True)
            o = _dotf(p.astype(BF16), vc_ref[0]) * (1.0 / l)
            oc_ref[0, pl.ds(rs, GRID_W), :] = unstack_heads(o).astype(BF16)
            return carry

        lax.fori_loop(0, nblk, ctx_body, 0)


def _na_bias_table(rpb, rows):
    wr = min(NA_ROWS, rows)
    qc = np.arange(GRID_W)
    kc = np.arange(GRID_W)
    dc = np.clip(kc[None, :] - qc[:, None], 1 - NA_COLS, NA_COLS - 1) + NA_COLS - 1
    ws = np.clip(qc - NA_COLS // 2, 0, GRID_W - NA_COLS)
    ok = (kc[None, :] >= ws[:, None]) & (kc[None, :] < ws[:, None] + NA_COLS)
    onehot = jnp.asarray(dc[None, :, :] == np.arange(2 * NA_COLS - 1)[:, None, None], F32)
    t = jnp.einsum('hrc,cqk->hrqk', rpb.astype(F32), onehot, precision=lax.Precision.HIGHEST)
    t = jnp.where(ok[None, None], t, -jnp.inf)
    t = jnp.stack([t[:, d0:d0 + wr] for d0 in range(NA_ROWS)], axis=0)
    return t.transpose(0, 1, 3, 2, 4).reshape(NA_ROWS, NA_HEADS * GRID_W, wr * GRID_W) * LOG2E


def _na(p, pc, q_gain, k_gain, rpb, ctx_out):
    bsz, n, _ = p.shape
    nc = pc.shape[1]
    rows = n // GRID_W
    w = NA_WIDTH
    bias = _na_bias_table(rpb, rows)
    bd = jnp.asarray(np.kron(np.eye(NA_HEADS), np.full((NA_HEAD_DIM, NA_HEAD_DIM), 1.0 / NA_HEAD_DIM)), BF16)
    qg = jnp.tile(q_gain.astype(F32), NA_HEADS).reshape(1, w)
    kg = jnp.tile(k_gain.astype(F32), NA_HEADS).reshape(1, w)
    col = lambda nrow, j: pl.BlockSpec((1, nrow, w), lambda b: (b, 0, j))
    const = lambda shape: pl.BlockSpec(shape, lambda b: (0,) * len(shape))
    in_specs = [col(n, 0), col(n, 1), col(n, 2)]
    args = [p, p, p]
    if ctx_out:
        in_specs.append(col(nc, 0))
        args.append(pc)
    in_specs += [col(nc, 1), col(nc, 2), const((1, w)), const((1, w)), const(bias.shape), const((w, w))]
    args += [pc, pc, qg, kg, bias, bd]
    out_specs = [pl.BlockSpec((1, n, w), lambda b: (b, 0, 0))]
    out_shape = [jax.ShapeDtypeStruct((bsz, n, w), BF16)]
    scratch = [pltpu.VMEM((n, w), BF16), pltpu.VMEM((n, w), BF16), pltpu.VMEM((nc, w), BF16)]
    if ctx_out:
        out_specs.append(pl.BlockSpec((1, nc, w), lambda b: (b, 0, 0)))
        out_shape.append(jax.ShapeDtypeStruct((bsz, nc, w), BF16))
        scratch.append(pltpu.VMEM((nc, w), BF16))
    outs = pl.pallas_call(
        functools.partial(_na_kernel, rows=rows, ctx_out=ctx_out),
        grid=(bsz,),
        in_specs=in_specs,
        out_specs=out_specs,
        out_shape=out_shape,
        scratch_shapes=scratch,
        compiler_params=_cparams(("parallel",)),
        name="na_attn",
    )(*args)
    return (outs[0], outs[1]) if ctx_out else (outs[0], None)


def _rope_tables(n):
    per_axis = DF_HEAD_DIM // 2
    inv_freq = ROPE_BASE ** (-jnp.arange(0, per_axis, 2, dtype=F32) / per_axis)
    t = jnp.arange(n, dtype=jnp.int32)
    pos = jnp.stack([t // GRID_W, t % GRID_W], axis=-1).astype(F32)
    ang = pos[:, :, None] * inv_freq
    ang = jnp.concatenate([ang, ang], axis=-1).reshape(n, DF_HEAD_DIM)
    reps = DF_WIDTH // DF_HEAD_DIM
    cos = jnp.tile(jnp.cos(ang), (1, reps))
    sin = jnp.tile(jnp.sin(ang), (1, reps))
    first = (np.arange(DF_WIDTH) % (DF_HEAD_DIM // 2)) < DF_HEAD_DIM // 4
    sin_a = jnp.where(first[None, :], -sin, 0.0)
    sin_b = jnp.where(first[None, :], 0.0, sin)
    return cos, sin_a, sin_b


def _df_kernel(*refs, ctx_out, lam_init, tq):
    if ctx_out:
        (q_ref, k_ref, v_ref, qc_ref, kc_ref, vc_ref, cos_ref, sa_ref, sb_ref, qg_ref, kg_ref, sub_ref,
         lam_ref, bd32_ref, bd64_ref, o_ref, oc_ref, qn_s, kall_s, vall_s, qcn_s) = refs
    else:
        (q_ref, k_ref, v_ref, kc_ref, vc_ref, cos_ref, sa_ref, sb_ref, qg_ref, kg_ref, sub_ref,
         lam_ref, bd32_ref, bd64_ref, o_ref, qn_s, kall_s, vall_s) = refs
    n = q_ref.shape[1]
    nc = kc_ref.shape[1]
    bd32 = bd32_ref[...]
    bd64 = bd64_ref[...]
    scale = DF_HEAD_DIM ** -0.5 * LOG2E
    lam = lam_ref[0, 0]
    shift = DF_HEAD_DIM // 4

    def norm(x, g):
        xf = x.astype(F32)
        return xf * lax.rsqrt(_segmean(xf * xf, bd32) + NORM_EPS) * g

    def rope(xn):
        return (xn * cos_ref[...] + pltpu.roll(xn, DF_WIDTH - shift, 1) * sa_ref[...]
                + pltpu.roll(xn, shift, 1) * sb_ref[...])

    qn_s[...] = (rope(norm(q_ref[0], qg_ref[...])) * scale).astype(BF16)
    kall_s[0:n, :] = rope(norm(k_ref[0], kg_ref[...])).astype(BF16)
    kall_s[n:n + nc, :] = norm(kc_ref[0], kg_ref[...]).astype(BF16)
    vall_s[0:n, :] = v_ref[0]
    vall_s[n:n + nc, :] = vc_ref[0]

    lane = lax.broadcasted_iota(jnp.int32, (1, DF_WIDTH), 1)
    comp = lane // DF_HEAD_DIM
    head = lane // (2 * DF_HEAD_DIM)
    sub_gain = sub_ref[...] * (1.0 - lam_init)

    def attend(q, kk, vv):
        t = q.shape[0]
        y = jnp.zeros((t, DF_WIDTH), F32)
        zero = jnp.zeros_like(q)
        for h in range(DF_HEADS):
            q2 = jnp.concatenate([jnp.where(comp == 2 * h, q, zero), jnp.where(comp == 2 * h + 1, q, zero)], axis=0)
            s = _dot_nt(q2, kk)
            m = jnp.max(s, axis=-1, keepdims=True)
            p = jnp.exp2(s - m)
            l = jnp.sum(p, axis=-1, keepdims=True)
            o = _dotf(p.astype(BF16), vv) * (1.0 / l)
            y = y + jnp.where(head == h, o[0:t] - lam * o[t:2 * t], 0.0)
        return y * lax.rsqrt(_segmean(y * y, bd64) + NORM_EPS) * sub_gain

    def q_body(i, carry):
        rs = pl.multiple_of(i * tq, tq)
        o_ref[0, pl.ds(rs, tq), :] = attend(qn_s[pl.ds(rs, tq), :], kall_s[...], vall_s[...]).astype(BF16)
        return carry

    lax.fori_loop(0, n // tq, q_body, 0)

    if ctx_out:
        qcn_s[...] = (norm(qc_ref[0], qg_ref[...]) * scale).astype(BF16)
        tc = min(tq, nc)

        def c_body(i, carry):
            rs = pl.multiple_of(i * tc, tc)
            oc_ref[0, pl.ds(rs, tc), :] = attend(qcn_s[pl.ds(rs, tc), :], kall_s[n:n + nc, :],
                                                 vall_s[n:n + nc, :]).astype(BF16)
            return carry

        lax.fori_loop(0, nc // tc, c_body, 0)


def _df(p, pc, tables, q_gain, k_gain, lam_vecs, subln, layer, ctx_out):
    bsz, n, _ = p.shape
    nc = pc.shape[1]
    w = DF_WIDTH
    cos, sin_a, sin_b = tables
    lam_init = 0.8 - 0.6 * math.exp(-0.3 * layer)
    lv = lam_vecs.astype(F32)
    lam = (jnp.exp(jnp.sum(lv[0] * lv[1])) - jnp.exp(jnp.sum(lv[2] * lv[3])) + lam_init).reshape(1, 1)
    bd32 = jnp.asarray(np.kron(np.eye(w // DF_HEAD_DIM), np.full((DF_HEAD_DIM,) * 2, 1.0 / DF_HEAD_DIM)), BF16)
    bd64 = jnp.asarray(np.kron(np.eye(DF_HEADS), np.full((2 * DF_HEAD_DIM,) * 2, 0.5 / DF_HEAD_DIM)), BF16)
    qg = jnp.tile(q_gain.astype(F32), w // DF_HEAD_DIM).reshape(1, w)
    kg = jnp.tile(k_gain.astype(F32), w // DF_HEAD_DIM).reshape(1, w)
    sub = jnp.tile(subln.astype(F32), DF_HEADS).reshape(1, w)
    col = lambda nrow, j: pl.BlockSpec((1, nrow, w), lambda b: (b, 0, j))
    const = lambda shape: pl.BlockSpec(shape, lambda b: (0,) * len(shape))
    in_specs = [col(n, 3), col(n, 4), col(n, 5)]
    args = [p, p, p]
    if ctx_out:
        in_specs.append(col(nc, 3))
        args.append(pc)
    in_specs += [col(nc, 4), col(nc, 5), const((n, w)), const((n, w)), const((n, w)),
                 const((1, w)), const((1, w)), const((1, w)),
                 pl.BlockSpec(memory_space=pltpu.SMEM), const((w, w)), const((w, w))]
    args += [pc, pc, cos, sin_a, sin_b, qg, kg, sub, lam, bd32, bd64]
    out_specs = [pl.BlockSpec((1, n, w), lambda b: (b, 0, 0))]
    out_shape = [jax.ShapeDtypeStruct((bsz, n, w), BF16)]
    scratch = [pltpu.VMEM((n, w), BF16), pltpu.VMEM((n + nc, w), BF16), pltpu.VMEM((n + nc, w), BF16)]
    if ctx_out:
        out_specs.append(pl.BlockSpec((1, nc, w), lambda b: (b, 0, 0)))
        out_shape.append(jax.ShapeDtypeStruct((bsz, nc, w), BF16))
        scratch.append(pltpu.VMEM((nc, w), BF16))
    outs = pl.pallas_call(
        functools.partial(_df_kernel, ctx_out=ctx_out, lam_init=lam_init, tq=DF_TQ),
        grid=(bsz,),
        in_specs=in_specs,
        out_specs=out_specs,
        out_shape=out_shape,
        scratch_shapes=scratch,
        compiler_params=_cparams(("parallel",)),
        name="diff_attn",
    )(*args)
    return (outs[0], outs[1]) if ctx_out else (outs[0], None)


def _ssd_kernel(*refs, ctx_out):
    if ctx_out:
        (z_ref, xbc_ref, dt_ref, zc_ref, xbcc_ref, dtc_ref, cw_ref, cb_ref, dtb_ref, alog_ref, dsk_ref, ng_ref,
         tri_ref, o_ref, oc_ref, xpad_s, u_s, g_s, dtv_s, y_s, st_s) = refs
    else:
        (z_ref, xbc_ref, dt_ref, xbcc_ref, dtc_ref, cw_ref, cb_ref, dtb_ref, alog_ref, dsk_ref, ng_ref,
         tri_ref, o_ref, xpad_s, u_s, g_s, dtv_s, y_s, st_s) = refs
        zc_ref = oc_ref = None
    n = xbc_ref.shape[1]
    nc = xbcc_ref.shape[1]
    q = SSM_CHUNK
    ncc = nc // q
    nch = (n + nc) // q
    pad = SUBLANE
    half = SSM_CONV // 2
    cdim = SSM_CONV_DIM
    lat0 = 2 * pad + nc

    zrow = jnp.zeros((pad, cdim), F32)
    xpad_s[0:pad, :] = zrow
    xpad_s[pad:pad + nc, :] = xbcc_ref[0].astype(F32)
    xpad_s[pad + nc:lat0, :] = zrow
    xpad_s[lat0:lat0 + n, :] = xbc_ref[0].astype(F32)
    xpad_s[lat0 + n:lat0 + n + pad, :] = zrow
    dtv_s[0:nc, :] = dtc_ref[0].astype(F32)
    dtv_s[nc:nc + n, :] = dt_ref[0].astype(F32)

    lane = lax.broadcasted_iota(jnp.int32, (1, LANE), 1)
    a_row = -jnp.exp(alog_ref[...])
    tri = tri_ref[...]
    win = q + 2 * pad

    def prep_body(c, carry):
        rb = pl.multiple_of(c * q + jnp.where(c >= ncc, pad, 0), SUBLANE)
        rows = pl.ds(pl.multiple_of(c * q, q), q)
        for s0 in range(0, cdim, CONV_STRIP):
            strip = slice(s0, s0 + CONV_STRIP)
            wv = xpad_s[pl.ds(rb, win), strip]
            acc = cb_ref[:, strip] + cw_ref[half:half + 1, strip] * wv[pad:pad + q]
            for k in range(SSM_CONV):
                if k != half:
                    acc = acc + cw_ref[k:k + 1, strip] * pltpu.roll(wv, (half - k) % win, 0)[pad:pad + q]
            u_s[rows, strip] = _silu(acc)
        raw = dtv_s[rows, :] + dtb_ref[...]
        dt = jnp.maximum(raw, 0.0) + jnp.log(1.0 + jnp.exp(-jnp.abs(raw)))
        dt = jnp.where(lane < 2 * SSM_HEADS, dt, 0.0)
        la = dt * a_row
        h3 = _split3(la)
        cs = _dotf(tri, h3[0]) + _dotf(tri, h3[1]) + _dotf(tri, h3[2])
        tot = cs[q - 1:q, :]
        g_s[rows, :] = jnp.where(lane < SSM_HEADS, cs, tot - cs + la)
        dtv_s[rows, :] = dt
        return carry

    lax.fori_loop(0, nch, prep_body, 0)

    ii = lax.broadcasted_iota(jnp.int32, (q, q), 0)
    jj = lax.broadcasted_iota(jnp.int32, (q, q), 1)
    hpg = SSM_HEADS // SSM_GROUPS
    gw = hpg * SSM_HEAD_DIM
    hl = lax.broadcasted_iota(jnp.int32, (1, gw), 1) // SSM_HEAD_DIM
    emask = [hl == e for e in range(hpg)]

    def chunk_update(c, d):
        rows = pl.ds(pl.multiple_of(c * q, q), q)
        g = g_s[rows, :]
        dtv = dtv_s[rows, :]
        tot = jnp.where(lane < SSM_HEADS, g[q - 1:q, :], g[0:1, :])
        eg = jnp.exp(g)
        etot = jnp.exp(tot)
        g_t = g.T
        w_t = (jnp.exp(tot - g) * dtv).T
        dt_t = dtv.T
        keep = (ii >= jj) if d == 0 else (jj >= ii)
        ys = []
        for grp in range(SSM_GROUPS):
            bm = u_s[rows, SSM_D_INNER + grp * SSM_STATE:SSM_D_INNER + (grp + 1) * SSM_STATE]
            cm = u_s[rows, SSM_D_INNER + (SSM_GROUPS + grp) * SSM_STATE:
                     SSM_D_INNER + (SSM_GROUPS + grp + 1) * SSM_STATE]
            xs = u_s[rows, grp * gw:(grp + 1) * gw].astype(BF16)
            prev = st_s[:, grp * gw:(grp + 1) * gw]
            xs_prev = jnp.concatenate([xs, prev.astype(BF16)], axis=0)
            cb = _dot_nt(cm.astype(BF16), bm.astype(BF16))
            bm_t = bm.T
            yg = jnp.zeros((q, gw), F32)
            sg = jnp.zeros((SSM_STATE, gw), F32)
            dec = jnp.zeros((1, gw), F32)
            for e in range(hpg):
                li = d * SSM_HEADS + grp * hpg + e
                gcol = jnp.broadcast_to(g[:, li:li + 1], (q, q))
                grow = jnp.broadcast_to(g_t[li:li + 1, :], (q, q))
                lm = jnp.exp(jnp.where(keep, gcol - grow, -jnp.inf))
                mm = cb * lm * jnp.broadcast_to(dt_t[li:li + 1, :], (q, q))
                l2 = jnp.broadcast_to(eg[:, li:li + 1], (q, q)) * cm
                yh = _dotf(jnp.concatenate([mm.astype(BF16), l2.astype(BF16)], axis=-1), xs_prev)
                yg = yg + jnp.where(emask[e], yh, 0.0)
                bw = bm_t * jnp.broadcast_to(w_t[li:li + 1, :], (q, q))
                sg = sg + jnp.where(emask[e], _dotf(bw.astype(BF16), xs), 0.0)
                dec = dec + jnp.where(emask[e], jnp.broadcast_to(etot[:, li:li + 1], (1, gw)), 0.0)
            st_s[:, grp * gw:(grp + 1) * gw] = prev * dec + sg
            ys.append(yg)
        return jnp.concatenate(ys, axis=-1)

    st_s[...] = jnp.zeros_like(st_s)

    def fwd_body(c, carry):
        y_s[pl.ds(pl.multiple_of(c * q, q), q), :] = chunk_update(c, 0)
        return carry

    lax.fori_loop(0, nch, fwd_body, 0)

    st_s[...] = jnp.zeros_like(st_s)

    def finish(c, yb, zz):
        rows = pl.ds(pl.multiple_of(c * q, q), q)
        y = y_s[rows, :] + yb + u_s[rows, 0:SSM_D_INNER] * dsk_ref[...]
        y = y * _silu(zz.astype(F32))
        return _rms_rows(y, ng_ref[...]).astype(BF16)

    def bwd_ctx_body(s, carry):
        c = ncc - 1 - s
        yb = chunk_update(c, 1)
        if ctx_out:
            rows = pl.ds(pl.multiple_of(c * q, q), q)
            oc_ref[0, rows, :] = finish(c, yb, zc_ref[0, rows, :])
        return carry

    lax.fori_loop(0, ncc, bwd_ctx_body, 0)

    def bwd_lat_body(s, carry):
        c = nch - 1 - s
        yb = chunk_update(c, 1)
        rows = pl.ds(pl.multiple_of((c - ncc) * q, q), q)
        o_ref[0, rows, :] = finish(c, yb, z_ref[0, rows, :])
        return carry

    lax.fori_loop(0, nch - ncc, bwd_lat_body, 0)


def _ssd(p, pc, conv_w, conv_b, dt_bias, a_log, d_skip, norm_gain, ctx_out):
    bsz, n, _ = p.shape
    nc = pc.shape[1]
    di = SSM_D_INNER
    q = SSM_CHUNK
    padl = lambda v: jnp.pad(v.astype(F32).reshape(1, -1), ((0, 0), (0, LANE - 2 * SSM_HEADS)))
    dtb = padl(dt_bias)
    alog = padl(a_log)
    dsk = jnp.repeat(d_skip.astype(F32), SSM_HEAD_DIM).reshape(1, di)
    ng = norm_gain.astype(F32).reshape(1, di)
    tri = jnp.asarray(np.tril(np.ones((q, q))), BF16)
    z_blk = (3 * NA_WIDTH + 3 * DF_WIDTH) // di
    x_blk = (3 * NA_WIDTH + 3 * DF_WIDTH + di) // SSM_CONV_DIM
    spec = lambda nrow, w, j: pl.BlockSpec((1, nrow, w), lambda b: (b, 0, j))
    const = lambda shape: pl.BlockSpec(shape, lambda b: (0,) * len(shape))
    in_specs = [spec(n, di, z_blk), spec(n, SSM_CONV_DIM, x_blk), spec(n, LANE, DT_BLOCK)]
    args = [p, p, p]
    if ctx_out:
        in_specs.append(spec(nc, di, z_blk))
        args.append(pc)
    in_specs += [spec(nc, SSM_CONV_DIM, x_blk), spec(nc, LANE, DT_BLOCK),
                 const((SSM_CONV, SSM_CONV_DIM)), const((1, SSM_CONV_DIM)), const((1, LANE)), const((1, LANE)),
                 const((1, di)), const((1, di)), const((q, q))]
    args += [pc, pc, conv_w.astype(F32), conv_b.astype(F32).reshape(1, -1), dtb, alog, dsk, ng, tri]
    out_specs = [pl.BlockSpec((1, n, di), lambda b: (b, 0, 0))]
    out_shape = [jax.ShapeDtypeStruct((bsz, n, di), BF16)]
    if ctx_out:
        out_specs.append(pl.BlockSpec((1, nc, di), lambda b: (b, 0, 0)))
        out_shape.append(jax.ShapeDtypeStruct((bsz, nc, di), BF16))
    tot = n + nc
    scratch = [pltpu.VMEM((tot + 3 * SUBLANE, SSM_CONV_DIM), F32), pltpu.VMEM((tot, SSM_CONV_DIM), F32),
               pltpu.VMEM((tot, LANE), F32), pltpu.VMEM((tot, LANE), F32), pltpu.VMEM((tot, di), F32),
               pltpu.VMEM((SSM_STATE, di), F32)]
    outs = pl.pallas_call(
        functools.partial(_ssd_kernel, ctx_out=ctx_out),
        grid=(bsz,),
        in_specs=in_specs,
        out_specs=out_specs,
        out_shape=out_shape,
        scratch_shapes=scratch,
        compiler_params=_cparams(("parallel",)),
        name="ssd",
    )(*args)
    return (outs[0], outs[1]) if ctx_out else (outs[0], None)


def _outproj_kernel(x_ref, a_ref, b_ref, c_ref, g_ref, w_ref, o_ref):
    y = jnp.concatenate([a_ref[0], b_ref[0], c_ref[0]], axis=-1)
    o_ref[0] = x_ref[0] + g_ref[0] * _dotf(y, w_ref[...])


def _outproj(x, y_na, y_df, y_ssm, mod, w, fixed_row):
    bsz, n, _ = x.shape
    tm = min(OUT_TM, n)
    row = lambda width: pl.BlockSpec((1, tm, width), lambda b, i: (b, i, 0))
    return pl.pallas_call(
        _outproj_kernel,
        grid=(bsz, n // tm),
        in_specs=[row(D_MODEL), row(NA_WIDTH), row(DF_WIDTH), row(SSM_D_INNER), _mod_spec(2, fixed_row),
                  pl.BlockSpec((D_MIX, D_MODEL), lambda b, i: (0, 0))],
        out_specs=row(D_MODEL),
        out_shape=jax.ShapeDtypeStruct((bsz, n, D_MODEL), F32),
        compiler_params=_cparams(("parallel", "parallel")),
        name="outproj",
    )(x, y_na, y_df, y_ssm, mod, w)


def _ffn_kernel(x_ref, xp_ref, xn_ref, sh_ref, sc_ref, g2_ref, gn_ref, wu_ref, cw_ref, cb_ref, wd_ref, o_ref,
                *, n_tiles):
    i = pl.program_id(1)
    tm = x_ref.shape[1]
    halo = xp_ref.shape[1]

    def modnorm(xf):
        return _rms_rows(xf, gn_ref[...]) * (1.0 + sc_ref[0]) + sh_ref[0]

    xm = x_ref[0]
    hm = modnorm(xm)
    hp = modnorm(xp_ref[0]) * jnp.where(i > 0, 1.0, 0.0)
    hn = modnorm(xn_ref[0]) * jnp.where(i < n_tiles - 1, 1.0, 0.0)
    h_ext = jnp.concatenate([hp, hm, hn], axis=0).astype(BF16)
    h_mid = hm.astype(BF16)
    ext = tm + 2 * halo
    acc = jnp.zeros((tm, D_MODEL), F32)
    for c in range(D_FF // FFN_CHUNK):
        lo = c * FFN_CHUNK
        gate = _dotf(h_ext, wu_ref[:, lo:lo + FFN_CHUNK])
        val = _dotf(h_mid, wu_ref[:, D_FF + lo:D_FF + lo + FFN_CHUNK])
        conv = (cb_ref[:, lo:lo + FFN_CHUNK]
                + cw_ref[0:1, lo:lo + FFN_CHUNK] * pltpu.roll(gate, 1, 0)[halo:halo + tm]
                + cw_ref[1:2, lo:lo + FFN_CHUNK] * gate[halo:halo + tm]
                + cw_ref[2:3, lo:lo + FFN_CHUNK] * pltpu.roll(gate, ext - 1, 0)[halo:halo + tm])
        act = (_silu(conv) * val).astype(BF16)
        acc = acc + _dotf(act, wd_ref[lo:lo + FFN_CHUNK, :])
    o_ref[0] = xm + g2_ref[0] * acc


def _ffn(x, mod, gain, w_up, conv_w, conv_b, w_down, fixed_row):
    bsz, n, _ = x.shape
    tm = min(FFN_TM, n)
    n_tiles = n // tm
    halo = SUBLANE
    hb = tm // halo
    last = n // halo - 1
    const = lambda shape: pl.BlockSpec(shape, lambda b, i: (0,) * len(shape), pipeline_mode=pl.Buffered(1))
    return pl.pallas_call(
        functools.partial(_ffn_kernel, n_tiles=n_tiles),
        grid=(bsz, n_tiles),
        in_specs=[
            pl.BlockSpec((1, tm, D_MODEL), lambda b, i: (b, i, 0)),
            pl.BlockSpec((1, halo, D_MODEL), lambda b, i: (b, jnp.maximum(i * hb - 1, 0), 0)),
            pl.BlockSpec((1, halo, D_MODEL), lambda b, i: (b, jnp.minimum((i + 1) * hb, last), 0)),
            _mod_spec(3, fixed_row),
            _mod_spec(4, fixed_row),
            _mod_spec(5, fixed_row),
            pl.BlockSpec((1, D_MODEL), lambda b, i: (0, 0)),
            const((D_MODEL, 2 * D_FF)),
            pl.BlockSpec((FFN_CONV, D_FF), lambda b, i: (0, 0)),
            pl.BlockSpec((1, D_FF), lambda b, i: (0, 0)),
            const((D_FF, D_MODEL)),
        ],
        out_specs=pl.BlockSpec((1, tm, D_MODEL), lambda b, i: (b, i, 0)),
        out_shape=jax.ShapeDtypeStruct((bsz, n, D_MODEL), F32),
        compiler_params=_cparams(("parallel", "parallel")),
        name="ffn",
    )(x, x, x, mod, mod, mod, gain, w_up, conv_w, conv_b, w_down)


def kernel(x, c, ctx, c_ctx, w_ada, b_ada, g_mix, g_ffn, w_in, na_q_gain, na_k_gain, na_rpb, df_q_gain, df_k_gain, df_lambda, df_subln, ssm_conv_w, ssm_conv_b, ssm_dt_bias, ssm_a_log, ssm_d, ssm_norm, w_out, ffn_w_up, ffn_conv_w, ffn_conv_b, ffn_w_down):
    bsz, n, _ = x.shape
    ctx_row = bsz
    mod_rows = -(-(bsz + 1) // SUBLANE) * SUBLANE
    cc = jnp.concatenate([c, c_ctx[None, :], jnp.zeros((mod_rows - bsz - 1, D_MODEL), c.dtype)], axis=0)
    mods = _ada(cc.astype(F32), w_ada.astype(F32), b_ada.astype(F32))
    mods = mods.reshape(DEPTH, mod_rows, 1, 6 * D_MODEL)
    tables = _rope_tables(n)
    w_in_b = jnp.pad(w_in, ((0, 0), (0, 0), (0, IN_COLS_PAD - IN_COLS))).astype(BF16)
    w_out_b = w_out.astype(BF16)
    w_up_b = ffn_w_up.astype(BF16)
    w_down_b = ffn_w_down.astype(BF16)
    x = x.astype(F32)
    ctx = ctx.astype(F32)
    for l in range(DEPTH):
        ctx_out = l < DEPTH - 1
        mod = mods[l]
        gm = g_mix[l].astype(F32).reshape(1, D_MODEL)
        gf = g_ffn[l].astype(F32).reshape(1, D_MODEL)
        p = _inproj(x, mod, gm, w_in_b[l], None)
        pc = _inproj(ctx, mod, gm, w_in_b[l], ctx_row)
        y_na, yc_na = _na(p, pc, na_q_gain[l], na_k_gain[l], na_rpb[l], ctx_out)
        y_df, yc_df = _df(p, pc, tables, df_q_gain[l], df_k_gain[l], df_lambda[l], df_subln[l], l, ctx_out)
        y_ssm, yc_ssm = _ssd(p, pc, ssm_conv_w[l], ssm_conv_b[l], ssm_dt_bias[l], ssm_a_log[l], ssm_d[l],
                             ssm_norm[l], ctx_out)
        cw = ffn_conv_w[l].astype(F32)
        cb = ffn_conv_b[l].astype(F32).reshape(1, D_FF)
        x = _outproj(x, y_na, y_df, y_ssm, mod, w_out_b[l], None)
        x = _ffn(x, mod, gf, w_up_b[l], cw, cb, w_down_b[l], None)
        if ctx_out:
            ctx = _outproj(ctx, yc_na, yc_df, yc_ssm, mod, w_out_b[l], ctx_row)
            ctx = _ffn(ctx, mod, gf, w_up_b[l], cw, cb, w_down_b[l], ctx_row)
    return x
```

```python
import functools
import math

import numpy as np
import jax
import jax.numpy as jnp
from jax import lax
from jax.experimental import pallas as pl
from jax.experimental.pallas import tpu as pltpu

F32 = jnp.float32
BF16 = jnp.bfloat16

D_MODEL = 1024
DEPTH = 2
GRID_W = 64
NORM_EPS = 1e-6
NA_HEADS = 4
NA_HEAD_DIM = 64
NA_ROWS = 8
NA_COLS = 16
NA_WIDTH = NA_HEADS * NA_HEAD_DIM
DF_HEADS = 4
DF_HEAD_DIM = 32
DF_WIDTH = DF_HEADS * 2 * DF_HEAD_DIM
ROPE_BASE = 10000.0
SSM_HEADS = 8
SSM_HEAD_DIM = 64
SSM_D_INNER = SSM_HEADS * SSM_HEAD_DIM
SSM_GROUPS = 2
SSM_STATE = 128
SSM_CONV = 5
SSM_CHUNK = 128
SSM_CONV_DIM = SSM_D_INNER + 2 * SSM_GROUPS * SSM_STATE
D_MIX = NA_WIDTH + DF_WIDTH + SSM_D_INNER
IN_COLS = 3 * NA_WIDTH + 3 * DF_WIDTH + SSM_D_INNER + SSM_CONV_DIM + 2 * SSM_HEADS
D_FF = 2816
FFN_CONV = 3

LANE = 128
SUBLANE = 8
IN_COLS_PAD = -(-IN_COLS // LANE) * LANE
DT_BLOCK = IN_COLS_PAD // LANE - 1
VMEM_LIMIT = 56 * 1024 * 1024

IN_TM = 512
OUT_TM = 512
FFN_TM = 512
FFN_CHUNK = D_FF // 2
DF_TQ = 128
DF_UNROLL = 2
DF_KBLOCK = 768
DF_KCHUNK = 128
DF_VT_ROWS =2 * DF_HEAD_DIM + 16
CONV_STRIP = 2 * LANE
NA_UNROLL = 4
LOG2E = math.log2(math.e)


def _cparams(sem):
    return pltpu.CompilerParams(dimension_semantics=sem, vmem_limit_bytes=VMEM_LIMIT)


def _dotf(a, b):
    return jnp.dot(a, b, preferred_element_type=F32)


def _dot_nt(a, b):
    return lax.dot_general(a, b, (((1,), (1,)), ((), ())), preferred_element_type=F32)


def _split2(x):
    hi = x.astype(BF16)
    lo = (x - hi.astype(F32)).astype(BF16)
    return hi, lo


def _split3(x):
    hi = x.astype(BF16)
    r = x - hi.astype(F32)
    mid = r.astype(BF16)
    lo = (r - mid.astype(F32)).astype(BF16)
    return hi, mid, lo


def _segmean(xsq, bd):
    hi, lo = _split2(xsq)
    return _dotf(hi, bd) + _dotf(lo, bd)


def _sigmoid(x):
    return 1.0 / (1.0 + jnp.exp(-x))


def _silu(x):
    return x * _sigmoid(x)


def _rms_rows(xf, gain):
    ms = jnp.mean(xf * xf, axis=-1, keepdims=True)
    return xf * lax.rsqrt(ms + NORM_EPS) * gain


def _ada_kernel(c_ref, w_ref, b_ref, o_ref):
    s = _silu(c_ref[...])
    shi, slo = _split2(s)
    whi, wlo = _split2(w_ref[0])
    o_ref[0] = _dotf(shi, whi) + _dotf(shi, wlo) + _dotf(slo, whi) + b_ref[0]


def _ada(cc, w_ada, b_ada):
    rows = cc.shape[0]
    nblk = 6
    return pl.pallas_call(
        _ada_kernel,
        grid=(DEPTH, nblk),
        in_specs=[
            pl.BlockSpec((rows, D_MODEL), lambda l, j: (0, 0)),
            pl.BlockSpec((1, D_MODEL, D_MODEL), lambda l, j: (l, 0, j)),
            pl.BlockSpec((1, 1, D_MODEL), lambda l, j: (l, 0, j)),
        ],
        out_specs=pl.BlockSpec((1, rows, D_MODEL), lambda l, j: (l, 0, j)),
        out_shape=jax.ShapeDtypeStruct((DEPTH, rows, 6 * D_MODEL), F32),
        compiler_params=_cparams(("parallel", "parallel")),
        name="adaln",
    )(cc, w_ada, b_ada.reshape(DEPTH, 1, 6 * D_MODEL))


def _mod_spec(chunk, fixed_row):
    if fixed_row is None:
        return pl.BlockSpec((1, 1, D_MODEL), lambda b, i: (b, 0, chunk))
    return pl.BlockSpec((1, 1, D_MODEL), lambda b, i: (fixed_row, 0, chunk))


def _inproj_kernel(x_ref, sh_ref, sc_ref, g_ref, w_ref, o_ref):
    h = _rms_rows(x_ref[0], g_ref[...]) * (1.0 + sc_ref[0]) + sh_ref[0]
    o_ref[0] = _dotf(h.astype(BF16), w_ref[...]).astype(BF16)


def _inproj(x, mod, gain, w, fixed_row):
    bsz, n, _ = x.shape
    tm = min(IN_TM, n)
    return pl.pallas_call(
        _inproj_kernel,
        grid=(bsz, n // tm),
        in_specs=[
            pl.BlockSpec((1, tm, D_MODEL), lambda b, i: (b, i, 0)),
            _mod_spec(0, fixed_row),
            _mod_spec(1, fixed_row),
            pl.BlockSpec((1, D_MODEL), lambda b, i: (0, 0)),
            pl.BlockSpec((D_MODEL, IN_COLS_PAD), lambda b, i: (0, 0)),
        ],
        out_specs=pl.BlockSpec((1, tm, IN_COLS_PAD), lambda b, i: (b, i, 0)),
        out_shape=jax.ShapeDtypeStruct((bsz, n, IN_COLS_PAD), BF16),
        compiler_params=_cparams(("parallel", "parallel")),
        name="inproj",
    )(x, mod, mod, gain, w)


def _na_kernel(*refs, rows, ctx_out):
    if ctx_out:
        (q_ref, k_ref, v_ref, qc_ref, kc_ref, vc_ref, qg_ref, kg_ref, bias_ref, bd_ref,
         o_ref, oc_ref, qn_s, kn_s, kcn_s, qcn_s) = refs
    else:
        (q_ref, k_ref, v_ref, kc_ref, vc_ref, qg_ref, kg_ref, bias_ref, bd_ref,
         o_ref, qn_s, kn_s, kcn_s) = refs
    bd = bd_ref[...]
    scale = NA_HEAD_DIM ** -0.5 * LOG2E

    def norm(x, g):
        xf = x.astype(F32)
        return xf * lax.rsqrt(_segmean(xf * xf, bd) + NORM_EPS) * g

    qn_s[...] = (norm(q_ref[0], qg_ref[...]) * scale).astype(BF16)
    kn_s[...] = norm(k_ref[0], kg_ref[...]).astype(BF16)
    kcn_s[...] = norm(kc_ref[0], kg_ref[...]).astype(BF16)

    head = lax.broadcasted_iota(jnp.int32, (1, NA_WIDTH), 1) // NA_HEAD_DIM
    hmask = [head == h for h in range(NA_HEADS)]
    wr = min(NA_ROWS, rows)
    nloc = wr * GRID_W

    def stack_heads(q):
        return jnp.concatenate([jnp.where(hmask[h], q, jnp.zeros_like(q)) for h in range(NA_HEADS)], axis=0)

    def unstack_heads(o):
        out = jnp.where(hmask[0], o[0:GRID_W], 0.0)
        for h in range(1, NA_HEADS):
            out = out + jnp.where(hmask[h], o[h * GRID_W:(h + 1) * GRID_W], 0.0)
        return out

    def row_body(r, carry):
        kr0 = jnp.clip(r - wr // 2, 0, rows - wr)
        d0 = kr0 - r + NA_ROWS - 1
        q4 = stack_heads(qn_s[pl.ds(pl.multiple_of(r * GRID_W, GRID_W), GRID_W), :])
        kstart = pl.multiple_of(kr0 * GRID_W, GRID_W)
        s_loc = _dot_nt(q4, kn_s[pl.ds(kstart, nloc), :]) + bias_ref[d0]
        s_ctx = _dot_nt(q4, kcn_s[...])
        m = jnp.maximum(jnp.max(s_loc, axis=-1, keepdims=True), jnp.max(s_ctx, axis=-1, keepdims=True))
        p_loc = jnp.exp2(s_loc - m)
        p_ctx = jnp.exp2(s_ctx - m)
        l = jnp.sum(p_loc, axis=-1, keepdims=True) + jnp.sum(p_ctx, axis=-1, keepdims=True)
        o = _dotf(p_loc.astype(BF16), v_ref[0, pl.ds(kstart, nloc), :]) + _dotf(p_ctx.astype(BF16), vc_ref[0])
        o = o * (1.0 / l)
        o_ref[0, pl.ds(pl.multiple_of(r * GRID_W, GRID_W), GRID_W), :] = unstack_heads(o).astype(BF16)
        return carry

    lax.fori_loop(0, rows, row_body, 0, unroll=NA_UNROLL)

    if ctx_out:
        qcn_s[...] = (norm(qc_ref[0], qg_ref[...]) * scale).astype(BF16)
        nblk = qcn_s.shape[0] // GRID_W

        def ctx_body(r, carry):
            rs = pl.multiple_of(r * GRID_W, GRID_W)
            q4 = stack_heads(qcn_s[pl.ds(rs, GRID_W), :])
            s = _dot_nt(q4, kcn_s[...])
            m = jnp.max(s, axis=-1, keepdims=True)
            p = jnp.exp2(s - m)
            l = jnp.sum(p, axis=-1, keepdims=True)
            o = _dotf(p.astype(BF16), vc_ref[0]) * (1.0 / l)
            oc_ref[0, pl.ds(rs, GRID_W), :] = unstack_heads(o).astype(BF16)
            return carry

        lax.fori_loop(0, nblk, ctx_body, 0)


def _na_bias_table(rpb, rows):
    wr = min(NA_ROWS, rows)
    qc = np.arange(GRID_W)
    kc = np.arange(GRID_W)
    dc = np.clip(kc[None, :] - qc[:, None], 1 - NA_COLS, NA_COLS - 1) + NA_COLS - 1
    ws = np.clip(qc - NA_COLS // 2, 0, GRID_W - NA_COLS)
    ok = (kc[None, :] >= ws[:, None]) & (kc[None, :] < ws[:, None] + NA_COLS)
    onehot = jnp.asarray(dc[None, :, :] == np.arange(2 * NA_COLS - 1)[:, None, None], F32)
    t = jnp.einsum('hrc,cqk->hrqk', rpb.astype(F32), onehot, precision=lax.Precision.HIGHEST)
    t = jnp.where(ok[None, None], t, -jnp.inf)
    t = jnp.stack([t[:, d0:d0 + wr] for d0 in range(NA_ROWS)], axis=0)
    return t.transpose(0, 1, 3, 2, 4).reshape(NA_ROWS, NA_HEADS * GRID_W, wr * GRID_W) * LOG2E


def _na(p, pc, q_gain, k_gain, rpb, ctx_out):
    bsz, n, _ = p.shape
    nc = pc.shape[1]
    rows = n // GRID_W
    w = NA_WIDTH
    bias = _na_bias_table(rpb, rows)
    bd = jnp.asarray(np.kron(np.eye(NA_HEADS), np.full((NA_HEAD_DIM, NA_HEAD_DIM), 1.0 / NA_HEAD_DIM)), BF16)
    qg = jnp.tile(q_gain.astype(F32), NA_HEADS).reshape(1, w)
    kg = jnp.tile(k_gain.astype(F32), NA_HEADS).reshape(1, w)
    col = lambda nrow, j: pl.BlockSpec((1, nrow, w), lambda b: (b, 0, j))
    const = lambda shape: pl.BlockSpec(shape, lambda b: (0,) * len(shape))
    in_specs = [col(n, 0), col(n, 1), col(n, 2)]
    args = [p, p, p]
    if ctx_out:
        in_specs.append(col(nc, 0))
        args.append(pc)
    in_specs += [col(nc, 1), col(nc, 2), const((1, w)), const((1, w)), const(bias.shape), const((w, w))]
    args += [pc, pc, qg, kg, bias, bd]
    out_specs = [pl.BlockSpec((1, n, w), lambda b: (b, 0, 0))]
    out_shape = [jax.ShapeDtypeStruct((bsz, n, w), BF16)]
    scratch = [pltpu.VMEM((n, w), BF16), pltpu.VMEM((n, w), BF16), pltpu.VMEM((nc, w), BF16)]
    if ctx_out:
        out_specs.append(pl.BlockSpec((1, nc, w), lambda b: (b, 0, 0)))
        out_shape.append(jax.ShapeDtypeStruct((bsz, nc, w), BF16))
        scratch.append(pltpu.VMEM((nc, w), BF16))
    outs = pl.pallas_call(
        functools.partial(_na_kernel, rows=rows, ctx_out=ctx_out),
        grid=(bsz,),
        in_specs=in_specs,
        out_specs=out_specs,
        out_shape=out_shape,
        scratch_shapes=scratch,
        compiler_params=_cparams(("parallel",)),
        name="na_attn",
    )(*args)
    return (outs[0], outs[1]) if ctx_out else (outs[0], None)


def _rope_tables(n):
    per_axis = DF_HEAD_DIM // 2
    inv_freq = ROPE_BASE ** (-jnp.arange(0, per_axis, 2, dtype=F32) / per_axis)
    t = jnp.arange(n, dtype=jnp.int32)
    pos = jnp.stack([t // GRID_W, t % GRID_W], axis=-1).astype(F32)
    ang = pos[:, :, None] * inv_freq
    ang = jnp.concatenate([ang, ang], axis=-1).reshape(n, DF_HEAD_DIM)
    reps = DF_WIDTH // DF_HEAD_DIM
    cos = jnp.tile(jnp.cos(ang), (1, reps))
    sin = jnp.tile(jnp.sin(ang), (1, reps))
    first = (np.arange(DF_WIDTH) % (DF_HEAD_DIM // 2)) < DF_HEAD_DIM // 4
    sin_a = jnp.where(first[None, :], -sin, 0.0)
    sin_b = jnp.where(first[None, :], 0.0, sin)
    return cos, sin_a, sin_b


def _df_kernel(*refs, ctx_out, lam_init, tq):
    if ctx_out:
        (q_ref, k_ref, v_ref, qc_ref, kc_ref, vc_ref, cos_ref, sa_ref, sb_ref, qg_ref, kg_ref, sub_ref,
         lam_ref, bd32_ref, bd64_ref, o_ref, oc_ref, qn_s, kall_s, vt_s, qcn_s) = refs
    else:
        (q_ref, k_ref, v_ref, kc_ref, vc_ref, cos_ref, sa_ref, sb_ref, qg_ref, kg_ref, sub_ref,
         lam_ref, bd32_ref, bd64_ref, o_ref, qn_s, kall_s, vt_s) = refs
    n = q_ref.shape[1]
    nc = kc_ref.shape[1]
    bd32 = bd32_ref[...]
    bd64 = bd64_ref[...]
    scale = DF_HEAD_DIM ** -0.5 * LOG2E
    lam = lam_ref[0, 0]
    shift = DF_HEAD_DIM // 4

    def norm(x, g):
        xf = x.astype(F32)
        return xf * lax.rsqrt(_segmean(xf * xf, bd32) + NORM_EPS) * g

    def rope(xn):
        return (xn * cos_ref[...] + pltpu.roll(xn, DF_WIDTH - shift, 1) * sa_ref[...]
                + pltpu.roll(xn, shift, 1) * sb_ref[...])

    qn_s[...] = (rope(norm(q_ref[0], qg_ref[...])) * scale).astype(BF16)
    kall_s[0:n, :] = rope(norm(k_ref[0], kg_ref[...])).astype(BF16)
    kall_s[n:n + nc, :] = norm(kc_ref[0], kg_ref[...]).astype(BF16)
    hd = 2 * DF_HEAD_DIM
    tail_row = lax.broadcasted_iota(jnp.int32, (DF_VT_ROWS - hd, n + nc), 0)
    tail = jnp.where(tail_row == 0, 1.0, 0.0).astype(BF16)
    v_t = v_ref[0].astype(F32).T
    vc_t = vc_ref[0].astype(F32).T
    for h in range(DF_HEADS):
        vt_s[h, 0:hd, 0:n] = v_t[h * hd:(h + 1) * hd].astype(BF16)
        vt_s[h, 0:hd, n:n + nc] = vc_t[h * hd:(h + 1) * hd].astype(BF16)
        vt_s[h, hd:DF_VT_ROWS, :] = tail

    lane = lax.broadcasted_iota(jnp.int32, (1, DF_WIDTH), 1)
    comp = lane // DF_HEAD_DIM
    sub_gain = sub_ref[...] * (1.0 - lam_init)

    def attend(q, k_lo, k_hi):
        t = q.shape[0]
        zero = jnp.zeros_like(q)
        nk = k_hi - k_lo
        blk = min(DF_KBLOCK, nk)

        def scores(h):
            q2 = jnp.concatenate([jnp.where(comp == 2 * h, q, zero), jnp.where(comp == 2 * h + 1, q, zero)], axis=0)
            return [_dot_nt(kall_s[k_lo + b0:k_lo + b0 + blk, :], q2) for b0 in range(0, nk, blk)]

        parts = []
        s_next = scores(0)
        for h in range(DF_HEADS):
            s_blocks = s_next
            if h + 1 < DF_HEADS:
                s_next = scores(h + 1)
            m = acc = None
            for bi, s_blk in enumerate(s_blocks):
                for c0 in range(0, blk, DF_KCHUNK):
                    s_c = s_blk[c0:c0 + DF_KCHUNK]
                    k0 = k_lo + bi * blk + c0
                    m_c = jnp.max(s_c, axis=0, keepdims=True)
                    m_new = m_c if m is None else jnp.maximum(m, m_c)
                    p_c = jnp.exp2((s_c - m_new).astype(BF16))
                    o_c = _dotf(vt_s[h, :, k0:k0 + DF_KCHUNK], p_c)
                    acc = o_c if acc is None else acc * jnp.exp2(m - m_new) + o_c
                    m = m_new
            o = acc[0:hd] * (1.0 / acc[hd:hd + 1])
            parts.append(o[:, 0:t] - lam * o[:, t:2 * t])
        y = jnp.concatenate(parts, axis=0).T
        return y * lax.rsqrt(_segmean(y * y, bd64) + NORM_EPS) * sub_gain

    def q_body(i, carry):
        rs = pl.multiple_of(i * tq, tq)
        o_ref[0, pl.ds(rs, tq), :] = attend(qn_s[pl.ds(rs, tq), :], 0, n + nc).astype(BF16)
        return carry

    lax.fori_loop(0, n // tq, q_body, 0, unroll=DF_UNROLL)

    if ctx_out:
        qcn_s[...] = (norm(qc_ref[0], qg_ref[...]) * scale).astype(BF16)
        tc = min(tq, nc)

        def c_body(i, carry):
            rs = pl.multiple_of(i * tc, tc)
            oc_ref[0, pl.ds(rs, tc), :] = attend(qcn_s[pl.ds(rs, tc), :], n, n + nc).astype(BF16)
            return carry

        lax.fori_loop(0, nc // tc, c_body, 0)


def _df(p, pc, tables, q_gain, k_gain, lam_vecs, subln, layer, ctx_out):
    bsz, n, _ = p.shape
    nc = pc.shape[1]
    w = DF_WIDTH
    cos, sin_a, sin_b = tables
    lam_init = 0.8 - 0.6 * math.exp(-0.3 * layer)
    lv = lam_vecs.astype(F32)
    lam = (jnp.exp(jnp.sum(lv[0] * lv[1])) - jnp.exp(jnp.sum(lv[2] * lv[3])) + lam_init).reshape(1, 1)
    bd32 = jnp.asarray(np.kron(np.eye(w // DF_HEAD_DIM), np.full((DF_HEAD_DIM,) * 2, 1.0 / DF_HEAD_DIM)), BF16)
    bd64 = jnp.asarray(np.kron(np.eye(DF_HEADS), np.full((2 * DF_HEAD_DIM,) * 2, 0.5 / DF_HEAD_DIM)), BF16)
    qg = jnp.tile(q_gain.astype(F32), w // DF_HEAD_DIM).reshape(1, w)
    kg = jnp.tile(k_gain.astype(F32), w // DF_HEAD_DIM).reshape(1, w)
    sub = jnp.tile(subln.astype(F32), DF_HEADS).reshape(1, w)
    col = lambda nrow, j: pl.BlockSpec((1, nrow, w), lambda b: (b, 0, j))
    const = lambda shape: pl.BlockSpec(shape, lambda b: (0,) * len(shape))
    in_specs = [col(n, 3), col(n, 4), col(n, 5)]
    args = [p, p, p]
    if ctx_out:
        in_specs.append(col(nc, 3))
        args.append(pc)
    in_specs += [col(nc, 4), col(nc, 5), const((n, w)), const((n, w)), const((n, w)),
                 const((1, w)), const((1, w)), const((1, w)),
                 pl.BlockSpec(memory_space=pltpu.SMEM), const((w, w)), const((w, w))]
    args += [pc, pc, cos, sin_a, sin_b, qg, kg, sub, lam, bd32, bd64]
    out_specs = [pl.BlockSpec((1, n, w), lambda b: (b, 0, 0))]
    out_shape = [jax.ShapeDtypeStruct((bsz, n, w), BF16)]
    scratch = [pltpu.VMEM((n, w), BF16), pltpu.VMEM((n + nc, w), BF16),
               pltpu.VMEM((DF_HEADS, DF_VT_ROWS, n + nc), BF16)]
    if ctx_out:
        out_specs.append(pl.BlockSpec((1, nc, w), lambda b: (b, 0, 0)))
        out_shape.append(jax.ShapeDtypeStruct((bsz, nc, w), BF16))
        scratch.append(pltpu.VMEM((nc, w), BF16))
    outs = pl.pallas_call(
        functools.partial(_df_kernel, ctx_out=ctx_out, lam_init=lam_init, tq=DF_TQ),
        grid=(bsz,),
        in_specs=in_specs,
        out_specs=out_specs,
        out_shape=out_shape,
        scratch_shapes=scratch,
        compiler_params=_cparams(("parallel",)),
        name="diff_attn",
    )(*args)
    return (outs[0], outs[1]) if ctx_out else (outs[0], None)


def _ssd_kernel(*refs, ctx_out):
    if ctx_out:
        (z_ref, xbc_ref, dt_ref, zc_ref, xbcc_ref, dtc_ref, cw_ref, cb_ref, dtb_ref, alog_ref, dsk_ref, ng_ref,
         tri_ref, o_ref, oc_ref, xpad_s, u_s, g_s, dtv_s, y_s, st_s, gt_s, wt_s, bmt_s) = refs
    else:
        (z_ref, xbc_ref, dt_ref, xbcc_ref, dtc_ref, cw_ref, cb_ref, dtb_ref, alog_ref, dsk_ref, ng_ref,
         tri_ref, o_ref, xpad_s, u_s, g_s, dtv_s, y_s, st_s, gt_s, wt_s, bmt_s) = refs
        zc_ref = oc_ref = None
    n = xbc_ref.shape[1]
    nc = xbcc_ref.shape[1]
    q = SSM_CHUNK
    ncc = nc // q
    nch = (n + nc) // q
    pad = SUBLANE
    half = SSM_CONV // 2
    cdim = SSM_CONV_DIM
    lat0 = 2 * pad + nc

    zrow = jnp.zeros((pad, cdim), F32)
    xpad_s[0:pad, :] = zrow
    xpad_s[pad:pad + nc, :] = xbcc_ref[0].astype(F32)
    xpad_s[pad + nc:lat0, :] = zrow
    xpad_s[lat0:lat0 + n, :] = xbc_ref[0].astype(F32)
    xpad_s[lat0 + n:lat0 + n + pad, :] = zrow
    dtv_s[0:nc, :] = dtc_ref[0].astype(F32)
    dtv_s[nc:nc + n, :] = dt_ref[0].astype(F32)

    lane = lax.broadcasted_iota(jnp.int32, (1, LANE), 1)
    a_row = -jnp.exp(alog_ref[...])
    tri = tri_ref[...]
    win = q + 2 * pad

    def prep_body(c, carry):
        rb = pl.multiple_of(c * q + jnp.where(c >= ncc, pad, 0), SUBLANE)
        rows = pl.ds(pl.multiple_of(c * q, q), q)
        for s0 in range(0, cdim, CONV_STRIP):
            strip = slice(s0, s0 + CONV_STRIP)
            wv = xpad_s[pl.ds(rb, win), strip]
            acc = cb_ref[:, strip] + cw_ref[half:half + 1, strip] * wv[pad:pad + q]
            for k in range(SSM_CONV):
                if k != half:
                    acc = acc + cw_ref[k:k + 1, strip] * pltpu.roll(wv, (half - k) % win, 0)[pad:pad + q]
            u_s[rows, strip] = _silu(acc)
        raw = dtv_s[rows, :] + dtb_ref[...]
        dt = jnp.maximum(raw, 0.0) + jnp.log(1.0 + jnp.exp(-jnp.abs(raw)))
        dt = jnp.where(lane < 2 * SSM_HEADS, dt, 0.0)
        la = dt * a_row
        h3 = _split3(la)
        cs = _dotf(tri, h3[0]) + _dotf(tri, h3[1]) + _dotf(tri, h3[2])
        tot = cs[q - 1:q, :]
        g = jnp.where(lane < SSM_HEADS, cs, tot - cs + la)
        g_s[rows, :] = g
        live = lane < 2 * SSM_HEADS
        glog = jnp.where(live, g - jnp.log(dt), 0.0)
        gt_s[c] = glog.T[0:2 * SSM_HEADS]
        wt_s[c] = jnp.where(live, jnp.exp(tot - g) * dt, 0.0).T[0:2 * SSM_HEADS]
        for grp in range(SSM_GROUPS):
            bmt_s[c, grp] = u_s[rows, SSM_D_INNER + grp * SSM_STATE:SSM_D_INNER + (grp + 1) * SSM_STATE].T
        return carry

    lax.fori_loop(0, nch, prep_body, 0)

    ii = lax.broadcasted_iota(jnp.int32, (q, q), 0)
    jj = lax.broadcasted_iota(jnp.int32, (q, q), 1)
    hpg = SSM_HEADS // SSM_GROUPS
    gw = hpg * SSM_HEAD_DIM
    hl = lax.broadcasted_iota(jnp.int32, (1, gw), 1) // SSM_HEAD_DIM
    emask = [hl == e for e in range(hpg)]

    def chunk_update(c, d):
        rows = pl.ds(pl.multiple_of(c * q, q), q)
        g = g_s[rows, :]
        tot = jnp.where(lane < SSM_HEADS, g[q - 1:q, :], g[0:1, :])
        eg = jnp.exp(g)
        etot = jnp.exp(tot)
        gl_t = gt_s[c]
        w_t = wt_s[c]
        keep = (ii >= jj) if d == 0 else (jj >= ii)
        ys = []
        for grp in range(SSM_GROUPS):
            bm = u_s[rows, SSM_D_INNER + grp * SSM_STATE:SSM_D_INNER + (grp + 1) * SSM_STATE]
            cm = u_s[rows, SSM_D_INNER + (SSM_GROUPS + grp) * SSM_STATE:
                     SSM_D_INNER + (SSM_GROUPS + grp + 1) * SSM_STATE]
            xs = u_s[rows, grp * gw:(grp + 1) * gw].astype(BF16)
            prev = st_s[:, grp * gw:(grp + 1) * gw]
            prev_b = prev.astype(BF16)
            cb = _dot_nt(cm.astype(BF16), bm.astype(BF16))
            bm_t = bmt_s[c, grp]
            zero = jnp.zeros_like(xs)
            lhs_y, rhs_y, lhs_s, rhs_s = [], [], [], []
            dec = jnp.zeros((1, gw), F32)
            for e in range(hpg):
                li = d * SSM_HEADS + grp * hpg + e
                gcol = jnp.broadcast_to(g[:, li:li + 1], (q, q))
                grow = jnp.broadcast_to(gl_t[li:li + 1, :], (q, q))
                mm = cb * jnp.exp(jnp.where(keep, gcol - grow, -jnp.inf))
                l2 = jnp.broadcast_to(eg[:, li:li + 1], (q, q)) * cm
                bw = bm_t * jnp.broadcast_to(w_t[li:li + 1, :], (q, q))
                xm = jnp.where(emask[e], xs, zero)
                pm = jnp.where(emask[e], prev_b, zero)
                lhs_y += [mm.astype(BF16), l2.astype(BF16)]
                rhs_y += [xm, pm]
                lhs_s.append(bw.astype(BF16))
                rhs_s.append(xm)
                dec = dec + jnp.where(emask[e], jnp.broadcast_to(etot[:, li:li + 1], (1, gw)), 0.0)
            ys.append(_dotf(jnp.concatenate(lhs_y, axis=-1), jnp.concatenate(rhs_y, axis=0)))
            sg = _dotf(jnp.concatenate(lhs_s, axis=-1), jnp.concatenate(rhs_s, axis=0))
            st_s[:, grp * gw:(grp + 1) * gw] = prev * dec + sg
        return jnp.concatenate(ys, axis=-1)

    st_s[...] = jnp.zeros_like(st_s)

    def fwd_body(c, carry):
        y_s[pl.ds(pl.multiple_of(c * q, q), q), :] = chunk_update(c, 0)
        return carry

    lax.fori_loop(0, nch, fwd_body, 0)

    st_s[...] = jnp.zeros_like(st_s)

    def finish(c, yb, zz):
        rows = pl.ds(pl.multiple_of(c * q, q), q)
        y = y_s[rows, :] + yb + u_s[rows, 0:SSM_D_INNER] * dsk_ref[...]
        y = y * _silu(zz.astype(F32))
        return _rms_rows(y, ng_ref[...]).astype(BF16)

    def bwd_ctx_body(s, carry):
        c = ncc - 1 - s
        yb = chunk_update(c, 1)
        if ctx_out:
            rows = pl.ds(pl.multiple_of(c * q, q), q)
            oc_ref[0, rows, :] = finish(c, yb, zc_ref[0, rows, :])
        return carry

    lax.fori_loop(0, ncc, bwd_ctx_body, 0)

    def bwd_lat_body(s, carry):
        c = nch - 1 - s
        yb = chunk_update(c, 1)
        rows = pl.ds(pl.multiple_of((c - ncc) * q, q), q)
        o_ref[0, rows, :] = finish(c, yb, z_ref[0, rows, :])
        return carry

    lax.fori_loop(0, nch - ncc, bwd_lat_body, 0)


def _ssd(p, pc, conv_w, conv_b, dt_bias, a_log, d_skip, norm_gain, ctx_out):
    bsz, n, _ = p.shape
    nc = pc.shape[1]
    di = SSM_D_INNER
    q = SSM_CHUNK
    padl = lambda v: jnp.pad(v.astype(F32).reshape(1, -1), ((0, 0), (0, LANE - 2 * SSM_HEADS)))
    dtb = padl(dt_bias)
    alog = padl(a_log)
    dsk = jnp.repeat(d_skip.astype(F32), SSM_HEAD_DIM).reshape(1, di)
    ng = norm_gain.astype(F32).reshape(1, di)
    tri = jnp.asarray(np.tril(np.ones((q, q))), BF16)
    z_blk = (3 * NA_WIDTH + 3 * DF_WIDTH) // di
    x_blk = (3 * NA_WIDTH + 3 * DF_WIDTH + di) // SSM_CONV_DIM
    spec = lambda nrow, w, j: pl.BlockSpec((1, nrow, w), lambda b: (b, 0, j))
    const = lambda shape: pl.BlockSpec(shape, lambda b: (0,) * len(shape))
    in_specs = [spec(n, di, z_blk), spec(n, SSM_CONV_DIM, x_blk), spec(n, LANE, DT_BLOCK)]
    args = [p, p, p]
    if ctx_out:
        in_specs.append(spec(nc, di, z_blk))
        args.append(pc)
    in_specs += [spec(nc, SSM_CONV_DIM, x_blk), spec(nc, LANE, DT_BLOCK),
                 const((SSM_CONV, SSM_CONV_DIM)), const((1, SSM_CONV_DIM)), const((1, LANE)), const((1, LANE)),
                 const((1, di)), const((1, di)), const((q, q))]
    args += [pc, pc, conv_w.astype(F32), conv_b.astype(F32).reshape(1, -1), dtb, alog, dsk, ng, tri]
    out_specs = [pl.BlockSpec((1, n, di), lambda b: (b, 0, 0))]
    out_shape = [jax.ShapeDtypeStruct((bsz, n, di), BF16)]
    if ctx_out:
        out_specs.append(pl.BlockSpec((1, nc, di), lambda b: (b, 0, 0)))
        out_shape.append(jax.ShapeDtypeStruct((bsz, nc, di), BF16))
    tot = n + nc
    scratch = [pltpu.VMEM((tot + 3 * SUBLANE, SSM_CONV_DIM), F32), pltpu.VMEM((tot, SSM_CONV_DIM), F32),
               pltpu.VMEM((tot, LANE), F32), pltpu.VMEM((tot, LANE), F32), pltpu.VMEM((tot, di), F32),
               pltpu.VMEM((SSM_STATE, di), F32),
               pltpu.VMEM((tot // q, 2 * SSM_HEADS, q), F32), pltpu.VMEM((tot // q, 2 * SSM_HEADS, q), F32),
               pltpu.VMEM((tot // q, SSM_GROUPS, SSM_STATE, q), F32)]
    outs = pl.pallas_call(
        functools.partial(_ssd_kernel, ctx_out=ctx_out),
        grid=(bsz,),
        in_specs=in_specs,
        out_specs=out_specs,
        out_shape=out_shape,
        scratch_shapes=scratch,
        compiler_params=_cparams(("parallel",)),
        name="ssd",
    )(*args)
    return (outs[0], outs[1]) if ctx_out else (outs[0], None)


def _outproj_kernel(x_ref, a_ref, b_ref, c_ref, g_ref, w_ref, o_ref):
    y = jnp.concatenate([a_ref[0], b_ref[0], c_ref[0]], axis=-1)
    o_ref[0] = x_ref[0] + g_ref[0] * _dotf(y, w_ref[...])


def _outproj(x, y_na, y_df, y_ssm, mod, w, fixed_row):
    bsz, n, _ = x.shape
    tm = min(OUT_TM, n)
    row = lambda width: pl.BlockSpec((1, tm, width), lambda b, i: (b, i, 0))
    return pl.pallas_call(
        _outproj_kernel,
        grid=(bsz, n // tm),
        in_specs=[row(D_MODEL), row(NA_WIDTH), row(DF_WIDTH), row(SSM_D_INNER), _mod_spec(2, fixed_row),
                  pl.BlockSpec((D_MIX, D_MODEL), lambda b, i: (0, 0))],
        out_specs=row(D_MODEL),
        out_shape=jax.ShapeDtypeStruct((bsz, n, D_MODEL), F32),
        compiler_params=_cparams(("parallel", "parallel")),
        name="outproj",
    )(x, y_na, y_df, y_ssm, mod, w)


def _ffn_kernel(x_ref, xp_ref, xn_ref, sh_ref, sc_ref, g2_ref, gn_ref, wu_ref, cw_ref, cb_ref, wd_ref, o_ref,
                *, n_tiles):
    i = pl.program_id(1)
    tm = x_ref.shape[1]
    halo = xp_ref.shape[1]

    def modnorm(xf):
        return _rms_rows(xf, gn_ref[...]) * (1.0 + sc_ref[0]) + sh_ref[0]

    xm = x_ref[0]
    hm = modnorm(xm)
    hp = modnorm(xp_ref[0]) * jnp.where(i > 0, 1.0, 0.0)
    hn = modnorm(xn_ref[0]) * jnp.where(i < n_tiles - 1, 1.0, 0.0)
    h_ext = jnp.concatenate([hp, hm, hn], axis=0).astype(BF16)
    h_mid = hm.astype(BF16)
    ext = tm + 2 * halo
    acc = jnp.zeros((tm, D_MODEL), F32)
    for c in range(D_FF // FFN_CHUNK):
        lo = c * FFN_CHUNK
        gate = _dotf(h_ext, wu_ref[:, lo:lo + FFN_CHUNK])
        val = _dotf(h_mid, wu_ref[:, D_FF + lo:D_FF + lo + FFN_CHUNK])
        conv = (cb_ref[:, lo:lo + FFN_CHUNK]
                + cw_ref[0:1, lo:lo + FFN_CHUNK] * pltpu.roll(gate, 1, 0)[halo:halo + tm]
                + cw_ref[1:2, lo:lo + FFN_CHUNK] * gate[halo:halo + tm]
                + cw_ref[2:3, lo:lo + FFN_CHUNK] * pltpu.roll(gate, ext - 1, 0)[halo:halo + tm])
        act = (_silu(conv) * val).astype(BF16)
        acc = acc + _dotf(act, wd_ref[lo:lo + FFN_CHUNK, :])
    o_ref[0] = xm + g2_ref[0] * acc


def _ffn(x, mod, gain, w_up, conv_w, conv_b, w_down, fixed_row):
    bsz, n, _ = x.shape
    tm = min(FFN_TM, n)
    n_tiles = n // tm
    halo = SUBLANE
    hb = tm // halo
    last = n // halo - 1
    const = lambda shape: pl.BlockSpec(shape, lambda b, i: (0,) * len(shape), pipeline_mode=pl.Buffered(1))
    return pl.pallas_call(
        functools.partial(_ffn_kernel, n_tiles=n_tiles),
        grid=(bsz, n_tiles),
        in_specs=[
            pl.BlockSpec((1, tm, D_MODEL), lambda b, i: (b, i, 0)),
            pl.BlockSpec((1, halo, D_MODEL), lambda b, i: (b, jnp.maximum(i * hb - 1, 0), 0)),
            pl.BlockSpec((1, halo, D_MODEL), lambda b, i: (b, jnp.minimum((i + 1) * hb, last), 0)),
            _mod_spec(3, fixed_row),
            _mod_spec(4, fixed_row),
            _mod_spec(5, fixed_row),
            pl.BlockSpec((1, D_MODEL), lambda b, i: (0, 0)),
            const((D_MODEL, 2 * D_FF)),
            pl.BlockSpec((FFN_CONV, D_FF), lambda b, i: (0, 0)),
            pl.BlockSpec((1, D_FF), lambda b, i: (0, 0)),
            const((D_FF, D_MODEL)),
        ],
        out_specs=pl.BlockSpec((1, tm, D_MODEL), lambda b, i: (b, i, 0)),
        out_shape=jax.ShapeDtypeStruct((bsz, n, D_MODEL), F32),
        compiler_params=_cparams(("parallel", "parallel")),
        name="ffn",
    )(x, x, x, mod, mod, mod, gain, w_up, conv_w, conv_b, w_down)


def kernel(x, c, ctx, c_ctx, w_ada, b_ada, g_mix, g_ffn, w_in, na_q_gain, na_k_gain, na_rpb, df_q_gain, df_k_gain, df_lambda, df_subln, ssm_conv_w, ssm_conv_b, ssm_dt_bias, ssm_a_log, ssm_d, ssm_norm, w_out, ffn_w_up, ffn_conv_w, ffn_conv_b, ffn_w_down):
    bsz, n, _ = x.shape
    ctx_row = bsz
    mod_rows = -(-(bsz + 1) // SUBLANE) * SUBLANE
    cc = jnp.concatenate([c, c_ctx[None, :], jnp.zeros((mod_rows - bsz - 1, D_MODEL), c.dtype)], axis=0)
    mods = _ada(cc.astype(F32), w_ada.astype(F32), b_ada.astype(F32))
    mods = mods.reshape(DEPTH, mod_rows, 1, 6 * D_MODEL)
    tables = _rope_tables(n)
    w_in_b = jnp.pad(w_in, ((0, 0), (0, 0), (0, IN_COLS_PAD - IN_COLS))).astype(BF16)
    w_out_b = w_out.astype(BF16)
    w_up_b = ffn_w_up.astype(BF16)
    w_down_b = ffn_w_down.astype(BF16)
    x = x.astype(F32)
    ctx = ctx.astype(F32)
    for l in range(DEPTH):
        ctx_out = l < DEPTH - 1
        mod = mods[l]
        gm = g_mix[l].astype(F32).reshape(1, D_MODEL)
        gf = g_ffn[l].astype(F32).reshape(1, D_MODEL)
        p = _inproj(x, mod, gm, w_in_b[l], None)
        pc = _inproj(ctx, mod, gm, w_in_b[l], ctx_row)
        y_na, yc_na = _na(p, pc, na_q_gain[l], na_k_gain[l], na_rpb[l], ctx_out)
        y_df, yc_df = _df(p, pc, tables, df_q_gain[l], df_k_gain[l], df_lambda[l], df_subln[l], l, ctx_out)
        y_ssm, yc_ssm = _ssd(p, pc, ssm_conv_w[l], ssm_conv_b[l], ssm_dt_bias[l], ssm_a_log[l], ssm_d[l],
                             ssm_norm[l], ctx_out)
        cw = ffn_conv_w[l].astype(F32)
        cb = ffn_conv_b[l].astype(F32).reshape(1, D_FF)
        x = _outproj(x, y_na, y_df, y_ssm, mod, w_out_b[l], None)
        x = _ffn(x, mod, gf, w_up_b[l], cw, cb, w_down_b[l], None)
        if ctx_out:
            ctx = _outproj(ctx, yc_na, yc_df, yc_ssm, mod, w_out_b[l], ctx_row)
            ctx = _ffn(ctx, mod, gf, w_up_b[l], cw, cb, w_down_b[l], ctx_row)
    return x
```

```python
import functools
import math

import numpy as np
import jax
import jax.numpy as jnp
from jax import lax
from jax.experimental import pallas as pl
from jax.experimental.pallas import tpu as pltpu

F32 = jnp.float32
BF16 = jnp.bfloat16

D_MODEL = 1024
DEPTH = 2
GRID_W = 64
NORM_EPS = 1e-6
NA_HEADS = 4
NA_HEAD_DIM = 64
NA_ROWS = 8
NA_COLS = 16
NA_WIDTH = NA_HEADS * NA_HEAD_DIM
DF_HEADS = 4
DF_HEAD_DIM = 32
DF_WIDTH = DF_HEADS * 2 * DF_HEAD_DIM
ROPE_BASE = 10000.0
SSM_HEADS = 8
SSM_HEAD_DIM = 64
SSM_D_INNER = SSM_HEADS * SSM_HEAD_DIM
SSM_GROUPS = 2
SSM_STATE = 128
SSM_CONV = 5
SSM_CHUNK = 128
SSM_CONV_DIM = SSM_D_INNER + 2 * SSM_GROUPS * SSM_STATE
D_MIX = NA_WIDTH + DF_WIDTH + SSM_D_INNER
IN_COLS = 3 * NA_WIDTH + 3 * DF_WIDTH + SSM_D_INNER + SSM_CONV_DIM + 2 * SSM_HEADS
D_FF = 2816
FFN_CONV = 3

LANE = 128
SUBLANE = 8
IN_COLS_PAD = -(-IN_COLS // LANE) * LANE
DT_BLOCK = IN_COLS_PAD // LANE - 1
VMEM_LIMIT = 56 * 1024 * 1024

IN_TM = 512
FFN_TM = 1024
FFN_CHUNKS = (1536, 1280)
MLP_HALO = 16
DF_TQ = 128
DF_UNROLL = 2
DF_KBLOCK = 768
DF_KCHUNK = 128
DF_VT_ROWS =2 * DF_HEAD_DIM + 16
CONV_STRIP = 2 * LANE
CONV_PAD = 16
CONV_WIN = 2 * SSM_CHUNK
NA_UNROLL = 8
LOG2E = math.log2(math.e)


def _cparams(sem):
    return pltpu.CompilerParams(dimension_semantics=sem, vmem_limit_bytes=VMEM_LIMIT)


def _dotf(a, b):
    return jnp.dot(a, b, preferred_element_type=F32)


def _dot_nt(a, b):
    return lax.dot_general(a, b, (((1,), (1,)), ((), ())), preferred_element_type=F32)


def _split2(x):
    hi = x.astype(BF16)
    lo = (x - hi.astype(F32)).astype(BF16)
    return hi, lo


def _split3(x):
    hi = x.astype(BF16)
    r = x - hi.astype(F32)
    mid = r.astype(BF16)
    lo = (r - mid.astype(F32)).astype(BF16)
    return hi, mid, lo


def _segmean(xsq, bd):
    hi, lo = _split2(xsq)
    return _dotf(hi, bd) + _dotf(lo, bd)


def _sigmoid(x):
    return 1.0 / (1.0 + jnp.exp(-x))


def _silu(x):
    return x * _sigmoid(x)


def _rms_rows(xf, gain):
    ms = jnp.mean(xf * xf, axis=-1, keepdims=True)
    return xf * lax.rsqrt(ms + NORM_EPS) * gain


def _ada_kernel(c_ref, w_ref, b_ref, o_ref):
    s = _silu(c_ref[...])
    shi, slo = _split2(s)
    whi, wlo = _split2(w_ref[0])
    o_ref[0] = _dotf(shi, whi) + _dotf(shi, wlo) + _dotf(slo, whi) + b_ref[0]


def _ada(cc, w_ada, b_ada):
    rows = cc.shape[0]
    nblk = 6
    return pl.pallas_call(
        _ada_kernel,
        grid=(DEPTH, nblk),
        in_specs=[
            pl.BlockSpec((rows, D_MODEL), lambda l, j: (0, 0)),
            pl.BlockSpec((1, D_MODEL, D_MODEL), lambda l, j: (l, 0, j)),
            pl.BlockSpec((1, 1, D_MODEL), lambda l, j: (l, 0, j)),
        ],
        out_specs=pl.BlockSpec((1, rows, D_MODEL), lambda l, j: (l, 0, j)),
        out_shape=jax.ShapeDtypeStruct((DEPTH, rows, 6 * D_MODEL), F32),
        compiler_params=_cparams(("parallel", "parallel")),
        name="adaln",
    )(cc, w_ada, b_ada.reshape(DEPTH, 1, 6 * D_MODEL))


def _mod_spec(chunk, fixed_row):
    if fixed_row is None:
        return pl.BlockSpec((1, 1, D_MODEL), lambda b, i: (b, 0, chunk))
    return pl.BlockSpec((1, 1, D_MODEL), lambda b, i: (fixed_row, 0, chunk))


def _inproj_kernel(x_ref, sh_ref, sc_ref, g_ref, w_ref, o_ref):
    h = _rms_rows(x_ref[0], g_ref[...]) * (1.0 + sc_ref[0]) + sh_ref[0]
    o_ref[0] = _dotf(h.astype(BF16), w_ref[...]).astype(BF16)


def _inproj(x, mod, gain, w, fixed_row):
    bsz, n, _ = x.shape
    tm = min(IN_TM, n)
    return pl.pallas_call(
        _inproj_kernel,
        grid=(bsz, n // tm),
        in_specs=[
            pl.BlockSpec((1, tm, D_MODEL), lambda b, i: (b, i, 0)),
            _mod_spec(0, fixed_row),
            _mod_spec(1, fixed_row),
            pl.BlockSpec((1, D_MODEL), lambda b, i: (0, 0)),
            pl.BlockSpec((D_MODEL, IN_COLS_PAD), lambda b, i: (0, 0)),
        ],
        out_specs=pl.BlockSpec((1, tm, IN_COLS_PAD), lambda b, i: (b, i, 0)),
        out_shape=jax.ShapeDtypeStruct((bsz, n, IN_COLS_PAD), BF16),
        compiler_params=_cparams(("parallel", "parallel")),
        name="inproj",
    )(x, mod, mod, gain, w)


def _na_kernel(*refs, rows, ctx_out):
    if ctx_out:
        (q_ref, k_ref, v_ref, qc_ref, kc_ref, vc_ref, qg_ref, kg_ref, bias_ref, bd_ref,
         o_ref, oc_ref, qn_s, kn_s, kcn_s, qcn_s) = refs
    else:
        (q_ref, k_ref, v_ref, kc_ref, vc_ref, qg_ref, kg_ref, bias_ref, bd_ref,
         o_ref, qn_s, kn_s, kcn_s) = refs
    bd = bd_ref[...]
    scale = NA_HEAD_DIM ** -0.5 * LOG2E

    def norm(x, g):
        xf = x.astype(F32)
        return xf * lax.rsqrt(_segmean(xf * xf, bd) + NORM_EPS) * g

    qn_s[...] = (norm(q_ref[0], qg_ref[...]) * scale).astype(BF16)
    kn_s[...] = norm(k_ref[0], kg_ref[...]).astype(BF16)
    kcn_s[...] = norm(kc_ref[0], kg_ref[...]).astype(BF16)

    head = lax.broadcasted_iota(jnp.int32, (1, NA_WIDTH), 1) // NA_HEAD_DIM
    hmask = [head == h for h in range(NA_HEADS)]
    wr = min(NA_ROWS, rows)
    nloc = wr * GRID_W

    def stack_heads(q):
        return jnp.concatenate([jnp.where(hmask[h], q, jnp.zeros_like(q)) for h in range(NA_HEADS)], axis=0)

    def unstack_heads(o):
        out = jnp.where(hmask[0], o[0:GRID_W], 0.0)
        for h in range(1, NA_HEADS):
            out = out + jnp.where(hmask[h], o[h * GRID_W:(h + 1) * GRID_W], 0.0)
        return out

    def row_body(r, carry):
        kr0 = jnp.clip(r - wr // 2, 0, rows - wr)
        d0 = kr0 - r + NA_ROWS - 1
        q4 = stack_heads(qn_s[pl.ds(pl.multiple_of(r * GRID_W, GRID_W), GRID_W), :])
        kstart = pl.multiple_of(kr0 * GRID_W, GRID_W)
        s_loc = _dot_nt(q4, kn_s[pl.ds(kstart, nloc), :]) + bias_ref[d0]
        s_ctx = _dot_nt(q4, kcn_s[...])
        m = jnp.maximum(jnp.max(s_loc, axis=-1, keepdims=True), jnp.max(s_ctx, axis=-1, keepdims=True))
        p_loc = jnp.exp2(s_loc - m)
        p_ctx = jnp.exp2(s_ctx - m)
        l = jnp.sum(p_loc, axis=-1, keepdims=True) + jnp.sum(p_ctx, axis=-1, keepdims=True)
        o = _dotf(p_loc.astype(BF16), v_ref[0, pl.ds(kstart, nloc), :]) + _dotf(p_ctx.astype(BF16), vc_ref[0])
        o = o * (1.0 / l)
        o_ref[0, pl.ds(pl.multiple_of(r * GRID_W, GRID_W), GRID_W), :] = unstack_heads(o).astype(BF16)
        return carry

    lax.fori_loop(0, rows, row_body, 0, unroll=NA_UNROLL)

    if ctx_out:
        qcn_s[...] = (norm(qc_ref[0], qg_ref[...]) * scale).astype(BF16)
        nblk = qcn_s.shape[0] // GRID_W

        def ctx_body(r, carry):
            rs = pl.multiple_of(r * GRID_W, GRID_W)
            q4 = stack_heads(qcn_s[pl.ds(rs, GRID_W), :])
            s = _dot_nt(q4, kcn_s[...])
            m = jnp.max(s, axis=-1, keepdims=True)
            p = jnp.exp2(s - m)
            l = jnp.sum(p, axis=-1, keepdims=True)
            o = _dotf(p.astype(BF16), vc_ref[0]) * (1.0 / l)
            oc_ref[0, pl.ds(rs, GRID_W), :] = unstack_heads(o).astype(BF16)
            return carry

        lax.fori_loop(0, nblk, ctx_body, 0)


def _na_bias_table(rpb, rows):
    wr = min(NA_ROWS, rows)
    qc = np.arange(GRID_W)
    kc = np.arange(GRID_W)
    dc = np.clip(kc[None, :] - qc[:, None], 1 - NA_COLS, NA_COLS - 1) + NA_COLS - 1
    ws = np.clip(qc - NA_COLS // 2, 0, GRID_W - NA_COLS)
    ok = (kc[None, :] >= ws[:, None]) & (kc[None, :] < ws[:, None] + NA_COLS)
    onehot = jnp.asarray(dc[None, :, :] == np.arange(2 * NA_COLS - 1)[:, None, None], F32)
    t = jnp.einsum('hrc,cqk->hrqk', rpb.astype(F32), onehot, precision=lax.Precision.HIGHEST)
    t = jnp.where(ok[None, None], t, -jnp.inf)
    t = jnp.stack([t[:, d0:d0 + wr] for d0 in range(NA_ROWS)], axis=0)
    return t.transpose(0, 1, 3, 2, 4).reshape(NA_ROWS, NA_HEADS * GRID_W, wr * GRID_W) * LOG2E


def _na(p, pc, q_gain, k_gain, rpb, ctx_out):
    bsz, n, _ = p.shape
    nc = pc.shape[1]
    rows = n // GRID_W
    w = NA_WIDTH
    bias = _na_bias_table(rpb, rows)
    bd = jnp.asarray(np.kron(np.eye(NA_HEADS), np.full((NA_HEAD_DIM, NA_HEAD_DIM), 1.0 / NA_HEAD_DIM)), BF16)
    qg = jnp.tile(q_gain.astype(F32), NA_HEADS).reshape(1, w)
    kg = jnp.tile(k_gain.astype(F32), NA_HEADS).reshape(1, w)
    col = lambda nrow, j: pl.BlockSpec((1, nrow, w), lambda b: (b, 0, j))
    const = lambda shape: pl.BlockSpec(shape, lambda b: (0,) * len(shape))
    in_specs = [col(n, 0), col(n, 1), col(n, 2)]
    args = [p, p, p]
    if ctx_out:
        in_specs.append(col(nc, 0))
        args.append(pc)
    in_specs += [col(nc, 1), col(nc, 2), const((1, w)), const((1, w)), const(bias.shape), const((w, w))]
    args += [pc, pc, qg, kg, bias, bd]
    out_specs = [pl.BlockSpec((1, n, w), lambda b: (b, 0, 0))]
    out_shape = [jax.ShapeDtypeStruct((bsz, n, w), BF16)]
    scratch = [pltpu.VMEM((n, w), BF16), pltpu.VMEM((n, w), BF16), pltpu.VMEM((nc, w), BF16)]
    if ctx_out:
        out_specs.append(pl.BlockSpec((1, nc, w), lambda b: (b, 0, 0)))
        out_shape.append(jax.ShapeDtypeStruct((bsz, nc, w), BF16))
        scratch.append(pltpu.VMEM((nc, w), BF16))
    outs = pl.pallas_call(
        functools.partial(_na_kernel, rows=rows, ctx_out=ctx_out),
        grid=(bsz,),
        in_specs=in_specs,
        out_specs=out_specs,
        out_shape=out_shape,
        scratch_shapes=scratch,
        compiler_params=_cparams(("parallel",)),
        name="na_attn",
    )(*args)
    return (outs[0], outs[1]) if ctx_out else (outs[0], None)


def _rope_tables(n):
    per_axis = DF_HEAD_DIM // 2
    inv_freq = ROPE_BASE ** (-jnp.arange(0, per_axis, 2, dtype=F32) / per_axis)
    t = jnp.arange(n, dtype=jnp.int32)
    pos = jnp.stack([t // GRID_W, t % GRID_W], axis=-1).astype(F32)
    ang = pos[:, :, None] * inv_freq
    ang = jnp.concatenate([ang, ang], axis=-1).reshape(n, DF_HEAD_DIM)
    reps = DF_WIDTH // DF_HEAD_DIM
    cos = jnp.tile(jnp.cos(ang), (1, reps))
    sin = jnp.tile(jnp.sin(ang), (1, reps))
    first = (np.arange(DF_WIDTH) % (DF_HEAD_DIM // 2)) < DF_HEAD_DIM // 4
    sin_a = jnp.where(first[None, :], -sin, 0.0)
    sin_b = jnp.where(first[None, :], 0.0, sin)
    return cos, sin_a, sin_b


def _df_kernel(*refs, ctx_out, lam_init, tq):
    if ctx_out:
        (q_ref, k_ref, v_ref, qc_ref, kc_ref, vc_ref, cos_ref, sa_ref, sb_ref, qg_ref, kg_ref, sub_ref,
         lam_ref, bd32_ref, bd64_ref, o_ref, oc_ref, qn_s, kall_s, vt_s, qcn_s) = refs
    else:
        (q_ref, k_ref, v_ref, kc_ref, vc_ref, cos_ref, sa_ref, sb_ref, qg_ref, kg_ref, sub_ref,
         lam_ref, bd32_ref, bd64_ref, o_ref, qn_s, kall_s, vt_s) = refs
    n = q_ref.shape[1]
    nc = kc_ref.shape[1]
    bd32 = bd32_ref[...]
    bd64 = bd64_ref[...]
    scale = DF_HEAD_DIM ** -0.5 * LOG2E
    lam = lam_ref[0, 0]
    shift = DF_HEAD_DIM // 4

    def norm(x, g):
        xf = x.astype(F32)
        return xf * lax.rsqrt(_segmean(xf * xf, bd32) + NORM_EPS) * g

    def rope(xn):
        return (xn * cos_ref[...] + pltpu.roll(xn, DF_WIDTH - shift, 1) * sa_ref[...]
                + pltpu.roll(xn, shift, 1) * sb_ref[...])

    qn_s[...] = (rope(norm(q_ref[0], qg_ref[...])) * scale).astype(BF16)
    kall_s[0:n, :] = rope(norm(k_ref[0], kg_ref[...])).astype(BF16)
    kall_s[n:n + nc, :] = norm(kc_ref[0], kg_ref[...]).astype(BF16)
    hd = 2 * DF_HEAD_DIM
    tail_row = lax.broadcasted_iota(jnp.int32, (DF_VT_ROWS - hd, n + nc), 0)
    tail = jnp.where(tail_row == 0, 1.0, 0.0).astype(BF16)
    v_t = v_ref[0].astype(F32).T
    vc_t = vc_ref[0].astype(F32).T
    for h in range(DF_HEADS):
        vt_s[h, 0:hd, 0:n] = v_t[h * hd:(h + 1) * hd].astype(BF16)
        vt_s[h, 0:hd, n:n + nc] = vc_t[h * hd:(h + 1) * hd].astype(BF16)
        vt_s[h, hd:DF_VT_ROWS, :] = tail

    lane = lax.broadcasted_iota(jnp.int32, (1, DF_WIDTH), 1)
    comp = lane // DF_HEAD_DIM
    sub_gain = sub_ref[...] * (1.0 - lam_init)

    def attend(q, k_lo, k_hi):
        t = q.shape[0]
        zero = jnp.zeros_like(q)
        nk = k_hi - k_lo
        blk = min(DF_KBLOCK, nk)

        def scores(h):
            q2 = jnp.concatenate([jnp.where(comp == 2 * h, q, zero), jnp.where(comp == 2 * h + 1, q, zero)], axis=0)
            return [_dot_nt(kall_s[k_lo + b0:k_lo + b0 + blk, :], q2) for b0 in range(0, nk, blk)]

        parts = []
        s_next = scores(0)
        for h in range(DF_HEADS):
            s_blocks = s_next
            if h + 1 < DF_HEADS:
                s_next = scores(h + 1)
            m = acc = None
            for bi, s_blk in enumerate(s_blocks):
                for c0 in range(0, blk, DF_KCHUNK):
                    s_c = s_blk[c0:c0 + DF_KCHUNK]
                    k0 = k_lo + bi * blk + c0
                    m_c = jnp.max(s_c, axis=0, keepdims=True)
                    m_new = m_c if m is None else jnp.maximum(m, m_c)
                    p_c = jnp.exp2((s_c - m_new).astype(BF16))
                    o_c = _dotf(vt_s[h, :, k0:k0 + DF_KCHUNK], p_c)
                    acc = o_c if acc is None else acc * jnp.exp2(m - m_new) + o_c
                    m = m_new
            o = acc[0:hd] * (1.0 / acc[hd:hd + 1])
            parts.append(o[:, 0:t] - lam * o[:, t:2 * t])
        y = jnp.concatenate(parts, axis=0).T
        return y * lax.rsqrt(_segmean(y * y, bd64) + NORM_EPS) * sub_gain

    def q_body(i, carry):
        rs = pl.multiple_of(i * tq, tq)
        o_ref[0, pl.ds(rs, tq), :] = attend(qn_s[pl.ds(rs, tq), :], 0, n + nc).astype(BF16)
        return carry

    lax.fori_loop(0, n // tq, q_body, 0, unroll=DF_UNROLL)

    if ctx_out:
        qcn_s[...] = (norm(qc_ref[0], qg_ref[...]) * scale).astype(BF16)
        tc = min(tq, nc)

        def c_body(i, carry):
            rs = pl.multiple_of(i * tc, tc)
            oc_ref[0, pl.ds(rs, tc), :] = attend(qcn_s[pl.ds(rs, tc), :], n, n + nc).astype(BF16)
            return carry

        lax.fori_loop(0, nc // tc, c_body, 0)


def _df(p, pc, tables, q_gain, k_gain, lam_vecs, subln, layer, ctx_out):
    bsz, n, _ = p.shape
    nc = pc.shape[1]
    w = DF_WIDTH
    cos, sin_a, sin_b = tables
    lam_init = 0.8 - 0.6 * math.exp(-0.3 * layer)
    lv = lam_vecs.astype(F32)
    lam = (jnp.exp(jnp.sum(lv[0] * lv[1])) - jnp.exp(jnp.sum(lv[2] * lv[3])) + lam_init).reshape(1, 1)
    bd32 = jnp.asarray(np.kron(np.eye(w // DF_HEAD_DIM), np.full((DF_HEAD_DIM,) * 2, 1.0 / DF_HEAD_DIM)), BF16)
    bd64 = jnp.asarray(np.kron(np.eye(DF_HEADS), np.full((2 * DF_HEAD_DIM,) * 2, 0.5 / DF_HEAD_DIM)), BF16)
    qg = jnp.tile(q_gain.astype(F32), w // DF_HEAD_DIM).reshape(1, w)
    kg = jnp.tile(k_gain.astype(F32), w // DF_HEAD_DIM).reshape(1, w)
    sub = jnp.tile(subln.astype(F32), DF_HEADS).reshape(1, w)
    col = lambda nrow, j: pl.BlockSpec((1, nrow, w), lambda b: (b, 0, j))
    const = lambda shape: pl.BlockSpec(shape, lambda b: (0,) * len(shape))
    in_specs = [col(n, 3), col(n, 4), col(n, 5)]
    args = [p, p, p]
    if ctx_out:
        in_specs.append(col(nc, 3))
        args.append(pc)
    in_specs += [col(nc, 4), col(nc, 5), const((n, w)), const((n, w)), const((n, w)),
                 const((1, w)), const((1, w)), const((1, w)),
                 pl.BlockSpec(memory_space=pltpu.SMEM), const((w, w)), const((w, w))]
    args += [pc, pc, cos, sin_a, sin_b, qg, kg, sub, lam, bd32, bd64]
    out_specs = [pl.BlockSpec((1, n, w), lambda b: (b, 0, 0))]
    out_shape = [jax.ShapeDtypeStruct((bsz, n, w), BF16)]
    scratch = [pltpu.VMEM((n, w), BF16), pltpu.VMEM((n + nc, w), BF16),
               pltpu.VMEM((DF_HEADS, DF_VT_ROWS, n + nc), BF16)]
    if ctx_out:
        out_specs.append(pl.BlockSpec((1, nc, w), lambda b: (b, 0, 0)))
        out_shape.append(jax.ShapeDtypeStruct((bsz, nc, w), BF16))
        scratch.append(pltpu.VMEM((nc, w), BF16))
    outs = pl.pallas_call(
        functools.partial(_df_kernel, ctx_out=ctx_out, lam_init=lam_init, tq=DF_TQ),
        grid=(bsz,),
        in_specs=in_specs,
        out_specs=out_specs,
        out_shape=out_shape,
        scratch_shapes=scratch,
        compiler_params=_cparams(("parallel",)),
        name="diff_attn",
    )(*args)
    return (outs[0], outs[1]) if ctx_out else (outs[0], None)


def _ssd_kernel(*refs, ctx_out):
    if ctx_out:
        (z_ref, xbc_ref, dt_ref, zc_ref, xbcc_ref, dtc_ref, cw_ref, cb_ref, dtb_ref, alog_ref, dsk_ref, ng_ref,
         tri_ref, shift_ref, o_ref, oc_ref, xpad_s, u_s, g_s, dtv_s, y_s, st_s, gt_s, wt_s, bmt_s) = refs
    else:
        (z_ref, xbc_ref, dt_ref, xbcc_ref, dtc_ref, cw_ref, cb_ref, dtb_ref, alog_ref, dsk_ref, ng_ref,
         tri_ref, shift_ref, o_ref, xpad_s, u_s, g_s, dtv_s, y_s, st_s, gt_s, wt_s, bmt_s) = refs
        zc_ref = oc_ref = None
    n = xbc_ref.shape[1]
    nc = xbcc_ref.shape[1]
    q = SSM_CHUNK
    ncc = nc // q
    nch = (n + nc) // q
    pad = CONV_PAD
    cdim = SSM_CONV_DIM
    lat0 = 2 * pad + nc

    xpad_s[0:pad, :] = jnp.zeros((pad, cdim), BF16)
    xpad_s[pad:pad + nc, :] = xbcc_ref[0]
    xpad_s[pad + nc:lat0, :] = jnp.zeros((pad, cdim), BF16)
    xpad_s[lat0:lat0 + n, :] = xbc_ref[0]
    xpad_s[lat0 + n:, :] = jnp.zeros((xpad_s.shape[0] - lat0 - n, cdim), BF16)
    dtv_s[0:nc, :] = dtc_ref[0].astype(F32)
    dtv_s[nc:nc + n, :] = dt_ref[0].astype(F32)

    lane = lax.broadcasted_iota(jnp.int32, (1, LANE), 1)
    a_row = -jnp.exp(alog_ref[...])
    tri = tri_ref[...]
    shifts = shift_ref[...]

    def prep_body(c, carry):
        rb = pl.multiple_of(c * q + jnp.where(c >= ncc, pad, 0), pad)
        rows = pl.ds(pl.multiple_of(c * q, q), q)
        for s0 in range(0, cdim, CONV_STRIP):
            strip = slice(s0, s0 + CONV_STRIP)
            taps = _dotf(shifts, xpad_s[pl.ds(rb, CONV_WIN), strip])
            acc = cb_ref[:, strip]
            for k in range(SSM_CONV):
                acc = acc + cw_ref[k:k + 1, strip] * taps[k * q:(k + 1) * q]
            u_s[rows, strip] = _silu(acc)
        raw = dtv_s[rows, :] + dtb_ref[...]
        dt = jnp.maximum(raw, 0.0) + jnp.log(1.0 + jnp.exp(-jnp.abs(raw)))
        dt = jnp.where(lane < 2 * SSM_HEADS, dt, 0.0)
        la = dt * a_row
        h3 = _split3(la)
        cs = _dotf(tri, h3[0]) + _dotf(tri, h3[1]) + _dotf(tri, h3[2])
        tot = cs[q - 1:q, :]
        g = jnp.where(lane < SSM_HEADS, cs, tot - cs + la)
        g_s[rows, :] = g
        live = lane < 2 * SSM_HEADS
        glog = jnp.where(live, g - jnp.log(dt), 0.0)
        gt_s[c] = glog.T[0:2 * SSM_HEADS]
        wt_s[c] = jnp.where(live, jnp.exp(tot - g) * dt, 0.0).T[0:2 * SSM_HEADS]
        for grp in range(SSM_GROUPS):
            bmt_s[c, grp] = u_s[rows, SSM_D_INNER + grp * SSM_STATE:SSM_D_INNER + (grp + 1) * SSM_STATE].T
        return carry

    lax.fori_loop(0, nch, prep_body, 0, unroll=2)

    ii = lax.broadcasted_iota(jnp.int32, (q, q), 0)
    jj = lax.broadcasted_iota(jnp.int32, (q, q), 1)
    hpg = SSM_HEADS // SSM_GROUPS
    gw = hpg * SSM_HEAD_DIM
    hl = lax.broadcasted_iota(jnp.int32, (1, gw), 1) // SSM_HEAD_DIM
    emask = [hl == e for e in range(hpg)]

    def chunk_update(c, d):
        rows = pl.ds(pl.multiple_of(c * q, q), q)
        g = g_s[rows, :]
        tot = jnp.where(lane < SSM_HEADS, g[q - 1:q, :], g[0:1, :])
        etot = jnp.exp(tot)
        gl_t = gt_s[c]
        w_t = wt_s[c]
        keep = (ii >= jj) if d == 0 else (jj >= ii)
        ys = []
        for grp in range(SSM_GROUPS):
            bm = u_s[rows, SSM_D_INNER + grp * SSM_STATE:SSM_D_INNER + (grp + 1) * SSM_STATE]
            cm = u_s[rows, SSM_D_INNER + (SSM_GROUPS + grp) * SSM_STATE:
                     SSM_D_INNER + (SSM_GROUPS + grp + 1) * SSM_STATE]
            xs = u_s[rows, grp * gw:(grp + 1) * gw].astype(BF16)
            prev = st_s[:, grp * gw:(grp + 1) * gw]
            prev_b = prev.astype(BF16)
            cb = _dot_nt(cm.astype(BF16), bm.astype(BF16))
            bm_t = bmt_s[c, grp]
            zero = jnp.zeros_like(xs)
            lhs_y, rhs_y, lhs_s, rhs_s = [], [], [], []
            dec = jnp.zeros((1, gw), F32)
            for e in range(hpg):
                li = d * SSM_HEADS + grp * hpg + e
                gcol = jnp.broadcast_to(g[:, li:li + 1], (q, q))
                grow = jnp.broadcast_to(gl_t[li:li + 1, :], (q, q))
                mm = cb * jnp.exp(jnp.where(keep, gcol - grow, -jnp.inf))
                l2 = jnp.exp(gcol) * cm
                bw = bm_t * jnp.broadcast_to(w_t[li:li + 1, :], (q, q))
                xm = jnp.where(emask[e], xs, zero)
                pm = jnp.where(emask[e], prev_b, zero)
                lhs_y += [mm.astype(BF16), l2.astype(BF16)]
                rhs_y += [xm, pm]
                lhs_s.append(bw.astype(BF16))
                rhs_s.append(xm)
                dec = dec + jnp.where(emask[e], jnp.broadcast_to(etot[:, li:li + 1], (1, gw)), 0.0)
            ys.append(_dotf(jnp.concatenate(lhs_y, axis=-1), jnp.concatenate(rhs_y, axis=0)))
            sg = _dotf(jnp.concatenate(lhs_s, axis=-1), jnp.concatenate(rhs_s, axis=0))
            st_s[:, grp * gw:(grp + 1) * gw] = prev * dec + sg
        return jnp.concatenate(ys, axis=-1)

    st_s[...] = jnp.zeros_like(st_s)

    def fwd_body(c, carry):
        y_s[pl.ds(pl.multiple_of(c * q, q), q), :] = chunk_update(c, 0)
        return carry

    lax.fori_loop(0, nch, fwd_body, 0, unroll=2)

    st_s[...] = jnp.zeros_like(st_s)

    def finish(c, yb, zz):
        rows = pl.ds(pl.multiple_of(c * q, q), q)
        y = y_s[rows, :] + yb + u_s[rows, 0:SSM_D_INNER] * dsk_ref[...]
        y = y * _silu(zz.astype(F32))
        return _rms_rows(y, ng_ref[...]).astype(BF16)

    def bwd_ctx_body(s, carry):
        c = ncc - 1 - s
        yb = chunk_update(c, 1)
        if ctx_out:
            rows = pl.ds(pl.multiple_of(c * q, q), q)
            oc_ref[0, rows, :] = finish(c, yb, zc_ref[0, rows, :])
        return carry

    lax.fori_loop(0, ncc, bwd_ctx_body, 0)

    def bwd_lat_body(s, carry):
        c = nch - 1 - s
        yb = chunk_update(c, 1)
        rows = pl.ds(pl.multiple_of((c - ncc) * q, q), q)
        o_ref[0, rows, :] = finish(c, yb, z_ref[0, rows, :])
        return carry

    lax.fori_loop(0, nch - ncc, bwd_lat_body, 0, unroll=2)


def _ssd(p, pc, conv_w, conv_b, dt_bias, a_log, d_skip, norm_gain, ctx_out):
    bsz, n, _ = p.shape
    nc = pc.shape[1]
    di = SSM_D_INNER
    q = SSM_CHUNK
    padl = lambda v: jnp.pad(v.astype(F32).reshape(1, -1), ((0, 0), (0, LANE - 2 * SSM_HEADS)))
    dtb = padl(dt_bias)
    alog = padl(a_log)
    dsk = jnp.repeat(d_skip.astype(F32), SSM_HEAD_DIM).reshape(1, di)
    ng = norm_gain.astype(F32).reshape(1, di)
    tri = jnp.asarray(np.tril(np.ones((q, q))), BF16)
    z_blk = (3 * NA_WIDTH + 3 * DF_WIDTH) // di
    x_blk = (3 * NA_WIDTH + 3 * DF_WIDTH + di) // SSM_CONV_DIM
    spec = lambda nrow, w, j: pl.BlockSpec((1, nrow, w), lambda b: (b, 0, j))
    const = lambda shape: pl.BlockSpec(shape, lambda b: (0,) * len(shape))
    in_specs = [spec(n, di, z_blk), spec(n, SSM_CONV_DIM, x_blk), spec(n, LANE, DT_BLOCK)]
    args = [p, p, p]
    if ctx_out:
        in_specs.append(spec(nc, di, z_blk))
        args.append(pc)
    in_specs += [spec(nc, SSM_CONV_DIM, x_blk), spec(nc, LANE, DT_BLOCK),
                 const((SSM_CONV, SSM_CONV_DIM)), const((1, SSM_CONV_DIM)), const((1, LANE)), const((1, LANE)),
                 const((1, di)), const((1, di)), const((q, q)), const((SSM_CONV * q, CONV_WIN))]
    sel = np.zeros((SSM_CONV * q, CONV_WIN), np.float32)
    for k in range(SSM_CONV):
        sel[k * q + np.arange(q), CONV_PAD + np.arange(q) + k - SSM_CONV // 2] = 1.0
    args += [pc, pc, conv_w.astype(F32), conv_b.astype(F32).reshape(1, -1), dtb, alog, dsk, ng, tri,
             jnp.asarray(sel, BF16)]
    out_specs = [pl.BlockSpec((1, n, di), lambda b: (b, 0, 0))]
    out_shape = [jax.ShapeDtypeStruct((bsz, n, di), BF16)]
    if ctx_out:
        out_specs.append(pl.BlockSpec((1, nc, di), lambda b: (b, 0, 0)))
        out_shape.append(jax.ShapeDtypeStruct((bsz, nc, di), BF16))
    tot = n + nc
    xpad_rows = tot - q + CONV_PAD + CONV_WIN
    scratch = [pltpu.VMEM((xpad_rows, SSM_CONV_DIM), BF16), pltpu.VMEM((tot, SSM_CONV_DIM), F32),
               pltpu.VMEM((tot, LANE), F32), pltpu.VMEM((tot, LANE), F32), pltpu.VMEM((tot, di), F32),
               pltpu.VMEM((SSM_STATE, di), F32),
               pltpu.VMEM((tot // q, 2 * SSM_HEADS, q), F32), pltpu.VMEM((tot // q, 2 * SSM_HEADS, q), F32),
               pltpu.VMEM((tot // q, SSM_GROUPS, SSM_STATE, q), F32)]
    outs = pl.pallas_call(
        functools.partial(_ssd_kernel, ctx_out=ctx_out),
        grid=(bsz,),
        in_specs=in_specs,
        out_specs=out_specs,
        out_shape=out_shape,
        scratch_shapes=scratch,
        compiler_params=_cparams(("parallel",)),
        name="ssd",
    )(*args)
    return (outs[0], outs[1]) if ctx_out else (outs[0], None)


def _mlp_kernel(x_ref, xp_ref, xn_ref, a_ref, ap_ref, an_ref, b_ref, bp_ref, bn_ref, c_ref, cp_ref, cn_ref,
                g1_ref, sh_ref, sc_ref, g2_ref, gn_ref, wo_ref, wu_ref, cw_ref, cb_ref, wd_ref, o_ref, *, n_tiles):
    i = pl.program_id(1)
    tm = x_ref.shape[1]
    halo = xp_ref.shape[1]
    ext = tm + 2 * halo
    y_ext = jnp.concatenate([
        jnp.concatenate([ap_ref[0], bp_ref[0], cp_ref[0]], axis=-1),
        jnp.concatenate([a_ref[0], b_ref[0], c_ref[0]], axis=-1),
        jnp.concatenate([an_ref[0], bn_ref[0], cn_ref[0]], axis=-1)], axis=0)
    x_ext = jnp.concatenate([xp_ref[0], x_ref[0], xn_ref[0]], axis=0)
    x1_ext = x_ext + g1_ref[0] * _dotf(y_ext, wo_ref[...])
    h = _rms_rows(x1_ext, gn_ref[...]) * (1.0 + sc_ref[0]) + sh_ref[0]
    r = lax.broadcasted_iota(jnp.int32, (ext, 1), 0)
    keep = jnp.where(r < halo, jnp.where(i > 0, 1.0, 0.0),
                     jnp.where(r >= halo + tm, jnp.where(i < n_tiles - 1, 1.0, 0.0), 1.0))
    h_ext = (h * keep).astype(BF16)
    h_mid = h_ext[halo:halo + tm]
    x1 = x1_ext[halo:halo + tm]
    acc = jnp.zeros((tm, D_MODEL), F32)
    lo = 0
    for width in FFN_CHUNKS:
        cols = slice(lo, lo + width)
        gate = _dotf(h_ext, wu_ref[:, cols])
        val = _dotf(h_mid, wu_ref[:, D_FF + lo:D_FF + lo + width])
        conv = (cb_ref[:, cols]
                + cw_ref[0:1, cols] * pltpu.roll(gate, 1, 0)[halo:halo + tm]
                + cw_ref[1:2, cols] * gate[halo:halo + tm]
                + cw_ref[2:3, cols] * pltpu.roll(gate, ext - 1, 0)[halo:halo + tm])
        act = (_silu(conv) * val).astype(BF16)
        acc = acc + _dotf(act, wd_ref[cols, :])
        lo += width
    o_ref[0] = x1 + g2_ref[0] * acc


def _mlp(x, y_na, y_df, y_ssm, mod, gain, w_out, w_up, conv_w, conv_b, w_down, fixed_row):
    bsz, n, _ = x.shape
    tm = min(FFN_TM, n)
    n_tiles = n // tm
    halo = MLP_HALO
    hb = tm // halo
    last = n // halo - 1
    const = lambda shape: pl.BlockSpec(shape, lambda b, i: (0,) * len(shape), pipeline_mode=pl.Buffered(1))

    def rows3(width):
        return [pl.BlockSpec((1, tm, width), lambda b, i: (b, i, 0)),
                pl.BlockSpec((1, halo, width), lambda b, i: (b, jnp.maximum(i * hb - 1, 0), 0)),
                pl.BlockSpec((1, halo, width), lambda b, i: (b, jnp.minimum((i + 1) * hb, last), 0))]

    return pl.pallas_call(
        functools.partial(_mlp_kernel, n_tiles=n_tiles),
        grid=(bsz, n_tiles),
        in_specs=rows3(D_MODEL) + rows3(NA_WIDTH) + rows3(DF_WIDTH) + rows3(SSM_D_INNER) + [
            _mod_spec(2, fixed_row),
            _mod_spec(3, fixed_row),
            _mod_spec(4, fixed_row),
            _mod_spec(5, fixed_row),
            pl.BlockSpec((1, D_MODEL), lambda b, i: (0, 0)),
            const((D_MIX, D_MODEL)),
            const((D_MODEL, 2 * D_FF)),
            pl.BlockSpec((FFN_CONV, D_FF), lambda b, i: (0, 0)),
            pl.BlockSpec((1, D_FF), lambda b, i: (0, 0)),
            const((D_FF, D_MODEL)),
        ],
        out_specs=pl.BlockSpec((1, tm, D_MODEL), lambda b, i: (b, i, 0)),
        out_shape=jax.ShapeDtypeStruct((bsz, n, D_MODEL), F32),
        compiler_params=_cparams(("parallel", "parallel")),
        name="mlp",
    )(x, x, x, y_na, y_na, y_na, y_df, y_df, y_df, y_ssm, y_ssm, y_ssm, mod, mod, mod, mod, gain,
      w_out, w_up, conv_w, conv_b, w_down)


def kernel(x, c, ctx, c_ctx, w_ada, b_ada, g_mix, g_ffn, w_in, na_q_gain, na_k_gain, na_rpb, df_q_gain, df_k_gain, df_lambda, df_subln, ssm_conv_w, ssm_conv_b, ssm_dt_bias, ssm_a_log, ssm_d, ssm_norm, w_out, ffn_w_up, ffn_conv_w, ffn_conv_b, ffn_w_down):
    bsz, n, _ = x.shape
    ctx_row = bsz
    mod_rows = -(-(bsz + 1) // SUBLANE) * SUBLANE
    cc = jnp.concatenate([c, c_ctx[None, :], jnp.zeros((mod_rows - bsz - 1, D_MODEL), c.dtype)], axis=0)
    mods = _ada(cc.astype(F32), w_ada.astype(F32), b_ada.astype(F32))
    mods = mods.reshape(DEPTH, mod_rows, 1, 6 * D_MODEL)
    tables = _rope_tables(n)
    w_in_b = jnp.pad(w_in, ((0, 0), (0, 0), (0, IN_COLS_PAD - IN_COLS))).astype(BF16)
    w_out_b = w_out.astype(BF16)
    w_up_b = ffn_w_up.astype(BF16)
    w_down_b = ffn_w_down.astype(BF16)
    x = x.astype(F32)
    ctx = ctx.astype(F32)
    for l in range(DEPTH):
        ctx_out = l < DEPTH - 1
        mod = mods[l]
        gm = g_mix[l].astype(F32).reshape(1, D_MODEL)
        gf = g_ffn[l].astype(F32).reshape(1, D_MODEL)
        p = _inproj(x, mod, gm, w_in_b[l], None)
        pc = _inproj(ctx, mod, gm, w_in_b[l], ctx_row)
        y_na, yc_na = _na(p, pc, na_q_gain[l], na_k_gain[l], na_rpb[l], ctx_out)
        y_df, yc_df = _df(p, pc, tables, df_q_gain[l], df_k_gain[l], df_lambda[l], df_subln[l], l, ctx_out)
        y_ssm, yc_ssm = _ssd(p, pc, ssm_conv_w[l], ssm_conv_b[l], ssm_dt_bias[l], ssm_a_log[l], ssm_d[l],
                             ssm_norm[l], ctx_out)
        cw = ffn_conv_w[l].astype(F32)
        cb = ffn_conv_b[l].astype(F32).reshape(1, D_FF)
        x = _mlp(x, y_na, y_df, y_ssm, mod, gf, w_out_b[l], w_up_b[l], cw, cb, w_down_b[l], None)
        if ctx_out:
            ctx = _mlp(ctx, yc_na, yc_df, yc_ssm, mod, gf, w_out_b[l], w_up_b[l], cw, cb, w_down_b[l], ctx_row)
    return x
```

```python
import functools
import math

import numpy as np
import jax
import jax.numpy as jnp
from jax import lax
from jax.experimental import pallas as pl
from jax.experimental.pallas import tpu as pltpu

F32 = jnp.float32
BF16 = jnp.bfloat16

D_MODEL = 1024
DEPTH = 2
GRID_W = 64
NORM_EPS = 1e-6
NA_HEADS = 4
NA_HEAD_DIM = 64
NA_ROWS = 8
NA_COLS = 16
NA_WIDTH = NA_HEADS * NA_HEAD_DIM
DF_HEADS = 4
DF_HEAD_DIM = 32
DF_WIDTH = DF_HEADS * 2 * DF_HEAD_DIM
ROPE_BASE = 10000.0
SSM_HEADS = 8
SSM_HEAD_DIM = 64
SSM_D_INNER = SSM_HEADS * SSM_HEAD_DIM
SSM_GROUPS = 2
SSM_STATE = 128
SSM_CONV = 5
SSM_CHUNK = 128
SSM_CONV_DIM = SSM_D_INNER + 2 * SSM_GROUPS * SSM_STATE
D_MIX = NA_WIDTH + DF_WIDTH + SSM_D_INNER
IN_COLS = 3 * NA_WIDTH + 3 * DF_WIDTH + SSM_D_INNER + SSM_CONV_DIM + 2 * SSM_HEADS
D_FF = 2816
FFN_CONV = 3

LANE = 128
SUBLANE = 8
IN_COLS_PAD = -(-IN_COLS // LANE) * LANE
DT_BLOCK = IN_COLS_PAD // LANE - 1
VMEM_LIMIT = 56 * 1024 * 1024

IN_TM = 1024
FFN_TM = 1024
FFN_CHUNKS = (1536, 1280)
MLP_HALO = 16
DF_TQ = 128
DF_UNROLL = 4
DF_KBLOCK = 768
DF_KCHUNK = 128
DF_VT_ROWS =2 * DF_HEAD_DIM + 16
CONV_STRIP = 2 * LANE
CONV_PAD = 16
CONV_WIN = 2 * SSM_CHUNK
NA_UNROLL = 8
LOG2E = math.log2(math.e)


def _cparams(sem):
    return pltpu.CompilerParams(dimension_semantics=sem, vmem_limit_bytes=VMEM_LIMIT)


def _dotf(a, b):
    return jnp.dot(a, b, preferred_element_type=F32)


def _dot_nt(a, b):
    return lax.dot_general(a, b, (((1,), (1,)), ((), ())), preferred_element_type=F32)


def _split2(x):
    hi = x.astype(BF16)
    lo = (x - hi.astype(F32)).astype(BF16)
    return hi, lo


def _split3(x):
    hi = x.astype(BF16)
    r = x - hi.astype(F32)
    mid = r.astype(BF16)
    lo = (r - mid.astype(F32)).astype(BF16)
    return hi, mid, lo


def _segmean(xsq, bd):
    hi, lo = _split2(xsq)
    return _dotf(hi, bd) + _dotf(lo, bd)


def _sigmoid(x):
    return 1.0 / (1.0 + jnp.exp(-x))


def _silu(x):
    return x * _sigmoid(x)


def _rms_rows(xf, gain):
    ms = jnp.mean(xf * xf, axis=-1, keepdims=True)
    return xf * lax.rsqrt(ms + NORM_EPS) * gain


def _ada_kernel(c_ref, w_ref, b_ref, o_ref):
    s = _silu(c_ref[...])
    shi, slo = _split2(s)
    whi, wlo = _split2(w_ref[0])
    o_ref[0] = _dotf(shi, whi) + _dotf(shi, wlo) + _dotf(slo, whi) + b_ref[0]


def _ada(cc, w_ada, b_ada):
    rows = cc.shape[0]
    nblk = 6
    return pl.pallas_call(
        _ada_kernel,
        grid=(DEPTH, nblk),
        in_specs=[
            pl.BlockSpec((rows, D_MODEL), lambda l, j: (0, 0)),
            pl.BlockSpec((1, D_MODEL, D_MODEL), lambda l, j: (l, 0, j)),
            pl.BlockSpec((1, 1, D_MODEL), lambda l, j: (l, 0, j)),
        ],
        out_specs=pl.BlockSpec((1, rows, D_MODEL), lambda l, j: (l, 0, j)),
        out_shape=jax.ShapeDtypeStruct((DEPTH, rows, 6 * D_MODEL), F32),
        compiler_params=_cparams(("parallel", "parallel")),
        name="adaln",
    )(cc, w_ada, b_ada.reshape(DEPTH, 1, 6 * D_MODEL))


def _mod_spec(chunk, fixed_row):
    if fixed_row is None:
        return pl.BlockSpec((1, 1, D_MODEL), lambda b, i: (b, 0, chunk))
    return pl.BlockSpec((1, 1, D_MODEL), lambda b, i: (fixed_row, 0, chunk))


def _inproj_kernel(x_ref, sh_ref, sc_ref, g_ref, w_ref, o_ref):
    h = _rms_rows(x_ref[0], g_ref[...]) * (1.0 + sc_ref[0]) + sh_ref[0]
    o_ref[0] = _dotf(h.astype(BF16), w_ref[...]).astype(BF16)


def _inproj(x, mod, gain, w, fixed_row):
    bsz, n, _ = x.shape
    tm = min(IN_TM, n)
    return pl.pallas_call(
        _inproj_kernel,
        grid=(bsz, n // tm),
        in_specs=[
            pl.BlockSpec((1, tm, D_MODEL), lambda b, i: (b, i, 0)),
            _mod_spec(0, fixed_row),
            _mod_spec(1, fixed_row),
            pl.BlockSpec((1, D_MODEL), lambda b, i: (0, 0)),
            pl.BlockSpec((D_MODEL, IN_COLS_PAD), lambda b, i: (0, 0)),
        ],
        out_specs=pl.BlockSpec((1, tm, IN_COLS_PAD), lambda b, i: (b, i, 0)),
        out_shape=jax.ShapeDtypeStruct((bsz, n, IN_COLS_PAD), BF16),
        compiler_params=_cparams(("parallel", "parallel")),
        name="inproj",
    )(x, mod, mod, gain, w)


def _na_kernel(*refs, rows, ctx_out):
    if ctx_out:
        (q_ref, k_ref, v_ref, qc_ref, kc_ref, vc_ref, qg_ref, kg_ref, bias_ref, bd_ref,
         o_ref, oc_ref, qn_s, kn_s, kcn_s, qcn_s) = refs
    else:
        (q_ref, k_ref, v_ref, kc_ref, vc_ref, qg_ref, kg_ref, bias_ref, bd_ref,
         o_ref, qn_s, kn_s, kcn_s) = refs
    bd = bd_ref[...]
    scale = NA_HEAD_DIM ** -0.5 * LOG2E

    def norm(x, g):
        xf = x.astype(F32)
        return xf * lax.rsqrt(_segmean(xf * xf, bd) + NORM_EPS) * g

    qn_s[...] = (norm(q_ref[0], qg_ref[...]) * scale).astype(BF16)
    kn_s[...] = norm(k_ref[0], kg_ref[...]).astype(BF16)
    kcn_s[...] = norm(kc_ref[0], kg_ref[...]).astype(BF16)

    head = lax.broadcasted_iota(jnp.int32, (1, NA_WIDTH), 1) // NA_HEAD_DIM
    hmask = [head == h for h in range(NA_HEADS)]
    wr = min(NA_ROWS, rows)
    nloc = wr * GRID_W

    def stack_heads(q):
        return jnp.concatenate([jnp.where(hmask[h], q, jnp.zeros_like(q)) for h in range(NA_HEADS)], axis=0)

    def unstack_heads(o):
        out = jnp.where(hmask[0], o[0:GRID_W], 0.0)
        for h in range(1, NA_HEADS):
            out = out + jnp.where(hmask[h], o[h * GRID_W:(h + 1) * GRID_W], 0.0)
        return out

    def row_body(r, carry):
        kr0 = jnp.clip(r - wr // 2, 0, rows - wr)
        d0 = kr0 - r + NA_ROWS - 1
        q4 = stack_heads(qn_s[pl.ds(pl.multiple_of(r * GRID_W, GRID_W), GRID_W), :])
        kstart = pl.multiple_of(kr0 * GRID_W, GRID_W)
        s_loc = _dot_nt(q4, kn_s[pl.ds(kstart, nloc), :]) + bias_ref[d0]
        s_ctx = _dot_nt(q4, kcn_s[...])
        m = jnp.maximum(jnp.max(s_loc, axis=-1, keepdims=True), jnp.max(s_ctx, axis=-1, keepdims=True))
        p_loc = jnp.exp2(s_loc - m)
        p_ctx = jnp.exp2(s_ctx - m)
        l = jnp.sum(p_loc, axis=-1, keepdims=True) + jnp.sum(p_ctx, axis=-1, keepdims=True)
        o = _dotf(p_loc.astype(BF16), v_ref[0, pl.ds(kstart, nloc), :]) + _dotf(p_ctx.astype(BF16), vc_ref[0])
        o = o * (1.0 / l)
        o_ref[0, pl.ds(pl.multiple_of(r * GRID_W, GRID_W), GRID_W), :] = unstack_heads(o).astype(BF16)
        return carry

    lax.fori_loop(0, rows, row_body, 0, unroll=NA_UNROLL)

    if ctx_out:
        qcn_s[...] = (norm(qc_ref[0], qg_ref[...]) * scale).astype(BF16)
        nblk = qcn_s.shape[0] // GRID_W

        def ctx_body(r, carry):
            rs = pl.multiple_of(r * GRID_W, GRID_W)
            q4 = stack_heads(qcn_s[pl.ds(rs, GRID_W), :])
            s = _dot_nt(q4, kcn_s[...])
            m = jnp.max(s, axis=-1, keepdims=True)
            p = jnp.exp2(s - m)
            l = jnp.sum(p, axis=-1, keepdims=True)
            o = _dotf(p.astype(BF16), vc_ref[0]) * (1.0 / l)
            oc_ref[0, pl.ds(rs, GRID_W), :] = unstack_heads(o).astype(BF16)
            return carry

        lax.fori_loop(0, nblk, ctx_body, 0)


def _na_bias_table(rpb, rows):
    wr = min(NA_ROWS, rows)
    qc = np.arange(GRID_W)
    kc = np.arange(GRID_W)
    dc = np.clip(kc[None, :] - qc[:, None], 1 - NA_COLS, NA_COLS - 1) + NA_COLS - 1
    ws = np.clip(qc - NA_COLS // 2, 0, GRID_W - NA_COLS)
    ok = (kc[None, :] >= ws[:, None]) & (kc[None, :] < ws[:, None] + NA_COLS)
    onehot = jnp.asarray(dc[None, :, :] == np.arange(2 * NA_COLS - 1)[:, None, None], F32)
    t = jnp.einsum('hrc,cqk->hrqk', rpb.astype(F32), onehot, precision=lax.Precision.HIGHEST)
    t = jnp.where(ok[None, None], t, -jnp.inf)
    t = jnp.stack([t[:, d0:d0 + wr] for d0 in range(NA_ROWS)], axis=0)
    return t.transpose(0, 1, 3, 2, 4).reshape(NA_ROWS, NA_HEADS * GRID_W, wr * GRID_W) * LOG2E


def _na(p, pc, q_gain, k_gain, rpb, ctx_out):
    bsz, n, _ = p.shape
    nc = pc.shape[1]
    rows = n // GRID_W
    w = NA_WIDTH
    bias = _na_bias_table(rpb, rows)
    bd = jnp.asarray(np.kron(np.eye(NA_HEADS), np.full((NA_HEAD_DIM, NA_HEAD_DIM), 1.0 / NA_HEAD_DIM)), BF16)
    qg = jnp.tile(q_gain.astype(F32), NA_HEADS).reshape(1, w)
    kg = jnp.tile(k_gain.astype(F32), NA_HEADS).reshape(1, w)
    col = lambda nrow, j: pl.BlockSpec((1, nrow, w), lambda b: (b, 0, j))
    const = lambda shape: pl.BlockSpec(shape, lambda b: (0,) * len(shape))
    in_specs = [col(n, 0), col(n, 1), col(n, 2)]
    args = [p, p, p]
    if ctx_out:
        in_specs.append(col(nc, 0))
        args.append(pc)
    in_specs += [col(nc, 1), col(nc, 2), const((1, w)), const((1, w)), const(bias.shape), const((w, w))]
    args += [pc, pc, qg, kg, bias, bd]
    out_specs = [pl.BlockSpec((1, n, w), lambda b: (b, 0, 0))]
    out_shape = [jax.ShapeDtypeStruct((bsz, n, w), BF16)]
    scratch = [pltpu.VMEM((n, w), BF16), pltpu.VMEM((n, w), BF16), pltpu.VMEM((nc, w), BF16)]
    if ctx_out:
        out_specs.append(pl.BlockSpec((1, nc, w), lambda b: (b, 0, 0)))
        out_shape.append(jax.ShapeDtypeStruct((bsz, nc, w), BF16))
        scratch.append(pltpu.VMEM((nc, w), BF16))
    outs = pl.pallas_call(
        functools.partial(_na_kernel, rows=rows, ctx_out=ctx_out),
        grid=(bsz,),
        in_specs=in_specs,
        out_specs=out_specs,
        out_shape=out_shape,
        scratch_shapes=scratch,
        compiler_params=_cparams(("parallel",)),
        name="na_attn",
    )(*args)
    return (outs[0], outs[1]) if ctx_out else (outs[0], None)


def _rope_tables(n):
    per_axis = DF_HEAD_DIM // 2
    inv_freq = ROPE_BASE ** (-jnp.arange(0, per_axis, 2, dtype=F32) / per_axis)
    t = jnp.arange(n, dtype=jnp.int32)
    pos = jnp.stack([t // GRID_W, t % GRID_W], axis=-1).astype(F32)
    ang = pos[:, :, None] * inv_freq
    ang = jnp.concatenate([ang, ang], axis=-1).reshape(n, DF_HEAD_DIM)
    reps = DF_WIDTH // DF_HEAD_DIM
    cos = jnp.tile(jnp.cos(ang), (1, reps))
    sin = jnp.tile(jnp.sin(ang), (1, reps))
    first = (np.arange(DF_WIDTH) % (DF_HEAD_DIM // 2)) < DF_HEAD_DIM // 4
    sin_a = jnp.where(first[None, :], -sin, 0.0)
    sin_b = jnp.where(first[None, :], 0.0, sin)
    return cos, sin_a, sin_b


def _df_kernel(*refs, ctx_out, lam_init, tq):
    if ctx_out:
        (q_ref, k_ref, v_ref, qc_ref, kc_ref, vc_ref, cos_ref, sa_ref, sb_ref, qg_ref, kg_ref, sub_ref,
         lam_ref, bd32_ref, bd64_ref, o_ref, oc_ref, qn_s, kall_s, vt_s, qcn_s) = refs
    else:
        (q_ref, k_ref, v_ref, kc_ref, vc_ref, cos_ref, sa_ref, sb_ref, qg_ref, kg_ref, sub_ref,
         lam_ref, bd32_ref, bd64_ref, o_ref, qn_s, kall_s, vt_s) = refs
    n = q_ref.shape[1]
    nc = kc_ref.shape[1]
    bd32 = bd32_ref[...]
    bd64 = bd64_ref[...]
    scale = DF_HEAD_DIM ** -0.5 * LOG2E
    lam = lam_ref[0, 0]
    shift = DF_HEAD_DIM // 4

    def norm(x, g):
        xf = x.astype(F32)
        return xf * lax.rsqrt(_segmean(xf * xf, bd32) + NORM_EPS) * g

    def rope(xn):
        return (xn * cos_ref[...] + pltpu.roll(xn, DF_WIDTH - shift, 1) * sa_ref[...]
                + pltpu.roll(xn, shift, 1) * sb_ref[...])

    qn_s[...] = (rope(norm(q_ref[0], qg_ref[...])) * scale).astype(BF16)
    kall_s[0:n, :] = rope(norm(k_ref[0], kg_ref[...])).astype(BF16)
    kall_s[n:n + nc, :] = norm(kc_ref[0], kg_ref[...]).astype(BF16)
    hd = 2 * DF_HEAD_DIM
    tail_row = lax.broadcasted_iota(jnp.int32, (DF_VT_ROWS - hd, n + nc), 0)
    tail = jnp.where(tail_row == 0, 1.0, 0.0).astype(BF16)
    v_t = v_ref[0].astype(F32).T
    vc_t = vc_ref[0].astype(F32).T
    for h in range(DF_HEADS):
        vt_s[h, 0:hd, 0:n] = v_t[h * hd:(h + 1) * hd].astype(BF16)
        vt_s[h, 0:hd, n:n + nc] = vc_t[h * hd:(h + 1) * hd].astype(BF16)
        vt_s[h, hd:DF_VT_ROWS, :] = tail

    lane = lax.broadcasted_iota(jnp.int32, (1, DF_WIDTH), 1)
    comp = lane // DF_HEAD_DIM
    sub_gain = sub_ref[...] * (1.0 - lam_init)

    def attend(q, k_lo, k_hi):
        t = q.shape[0]
        zero = jnp.zeros_like(q)
        nk = k_hi - k_lo
        blk = min(DF_KBLOCK, nk)

        def scores(h):
            q2 = jnp.concatenate([jnp.where(comp == 2 * h, q, zero), jnp.where(comp == 2 * h + 1, q, zero)], axis=0)
            return [_dot_nt(kall_s[k_lo + b0:k_lo + b0 + blk, :], q2) for b0 in range(0, nk, blk)]

        parts = []
        s_next = scores(0)
        for h in range(DF_HEADS):
            s_blocks = s_next
            if h + 1 < DF_HEADS:
                s_next = scores(h + 1)
            m = acc = None
            for bi, s_blk in enumerate(s_blocks):
                for c0 in range(0, blk, DF_KCHUNK):
                    s_c = s_blk[c0:c0 + DF_KCHUNK]
                    k0 = k_lo + bi * blk + c0
                    m_c = jnp.max(s_c, axis=0, keepdims=True)
                    m_new = m_c if m is None else jnp.maximum(m, m_c)
                    p_c = jnp.exp2((s_c - m_new).astype(BF16))
                    o_c = _dotf(vt_s[h, :, k0:k0 + DF_KCHUNK], p_c)
                    acc = o_c if acc is None else acc * jnp.exp2(m - m_new) + o_c
                    m = m_new
            o = acc[0:hd] * (1.0 / acc[hd:hd + 1])
            parts.append(o[:, 0:t] - lam * o[:, t:2 * t])
        y = jnp.concatenate(parts, axis=0).T
        return y * lax.rsqrt(_segmean(y * y, bd64) + NORM_EPS) * sub_gain

    def q_body(i, carry):
        rs = pl.multiple_of(i * tq, tq)
        o_ref[0, pl.ds(rs, tq), :] = attend(qn_s[pl.ds(rs, tq), :], 0, n + nc).astype(BF16)
        return carry

    lax.fori_loop(0, n // tq, q_body, 0, unroll=DF_UNROLL)

    if ctx_out:
        qcn_s[...] = (norm(qc_ref[0], qg_ref[...]) * scale).astype(BF16)
        tc = min(tq, nc)

        def c_body(i, carry):
            rs = pl.multiple_of(i * tc, tc)
            oc_ref[0, pl.ds(rs, tc), :] = attend(qcn_s[pl.ds(rs, tc), :], n, n + nc).astype(BF16)
            return carry

        lax.fori_loop(0, nc // tc, c_body, 0)


def _df(p, pc, tables, q_gain, k_gain, lam_vecs, subln, layer, ctx_out):
    bsz, n, _ = p.shape
    nc = pc.shape[1]
    w = DF_WIDTH
    cos, sin_a, sin_b = tables
    lam_init = 0.8 - 0.6 * math.exp(-0.3 * layer)
    lv = lam_vecs.astype(F32)
    lam = (jnp.exp(jnp.sum(lv[0] * lv[1])) - jnp.exp(jnp.sum(lv[2] * lv[3])) + lam_init).reshape(1, 1)
    bd32 = jnp.asarray(np.kron(np.eye(w // DF_HEAD_DIM), np.full((DF_HEAD_DIM,) * 2, 1.0 / DF_HEAD_DIM)), BF16)
    bd64 = jnp.asarray(np.kron(np.eye(DF_HEADS), np.full((2 * DF_HEAD_DIM,) * 2, 0.5 / DF_HEAD_DIM)), BF16)
    qg = jnp.tile(q_gain.astype(F32), w // DF_HEAD_DIM).reshape(1, w)
    kg = jnp.tile(k_gain.astype(F32), w // DF_HEAD_DIM).reshape(1, w)
    sub = jnp.tile(subln.astype(F32), DF_HEADS).reshape(1, w)
    col = lambda nrow, j: pl.BlockSpec((1, nrow, w), lambda b: (b, 0, j))
    const = lambda shape: pl.BlockSpec(shape, lambda b: (0,) * len(shape))
    in_specs = [col(n, 3), col(n, 4), col(n, 5)]
    args = [p, p, p]
    if ctx_out:
        in_specs.append(col(nc, 3))
        args.append(pc)
    in_specs += [col(nc, 4), col(nc, 5), const((n, w)), const((n, w)), const((n, w)),
                 const((1, w)), const((1, w)), const((1, w)),
                 pl.BlockSpec(memory_space=pltpu.SMEM), const((w, w)), const((w, w))]
    args += [pc, pc, cos, sin_a, sin_b, qg, kg, sub, lam, bd32, bd64]
    out_specs = [pl.BlockSpec((1, n, w), lambda b: (b, 0, 0))]
    out_shape = [jax.ShapeDtypeStruct((bsz, n, w), BF16)]
    scratch = [pltpu.VMEM((n, w), BF16), pltpu.VMEM((n + nc, w), BF16),
               pltpu.VMEM((DF_HEADS, DF_VT_ROWS, n + nc), BF16)]
    if ctx_out:
        out_specs.append(pl.BlockSpec((1, nc, w), lambda b: (b, 0, 0)))
        out_shape.append(jax.ShapeDtypeStruct((bsz, nc, w), BF16))
        scratch.append(pltpu.VMEM((nc, w), BF16))
    outs = pl.pallas_call(
        functools.partial(_df_kernel, ctx_out=ctx_out, lam_init=lam_init, tq=DF_TQ),
        grid=(bsz,),
        in_specs=in_specs,
        out_specs=out_specs,
        out_shape=out_shape,
        scratch_shapes=scratch,
        compiler_params=_cparams(("parallel",)),
        name="diff_attn",
    )(*args)
    return (outs[0], outs[1]) if ctx_out else (outs[0], None)


def _ssd_kernel(*refs, ctx_out):
    if ctx_out:
        (z_ref, xbc_ref, dt_ref, zc_ref, xbcc_ref, dtc_ref, cw_ref, cb_ref, dtb_ref, alog_ref, dsk_ref, ng_ref,
         tri_ref, shift_ref, o_ref, oc_ref, xpad_s, u_s, g_s, dtv_s, y_s, st_s, gt_s, wt_s, bmt_s) = refs
    else:
        (z_ref, xbc_ref, dt_ref, xbcc_ref, dtc_ref, cw_ref, cb_ref, dtb_ref, alog_ref, dsk_ref, ng_ref,
         tri_ref, shift_ref, o_ref, xpad_s, u_s, g_s, dtv_s, y_s, st_s, gt_s, wt_s, bmt_s) = refs
        zc_ref = oc_ref = None
    n = xbc_ref.shape[1]
    nc = xbcc_ref.shape[1]
    q = SSM_CHUNK
    ncc = nc // q
    nch = (n + nc) // q
    pad = CONV_PAD
    cdim = SSM_CONV_DIM
    lat0 = 2 * pad + nc

    xpad_s[0:pad, :] = jnp.zeros((pad, cdim), BF16)
    xpad_s[pad:pad + nc, :] = xbcc_ref[0]
    xpad_s[pad + nc:lat0, :] = jnp.zeros((pad, cdim), BF16)
    xpad_s[lat0:lat0 + n, :] = xbc_ref[0]
    xpad_s[lat0 + n:, :] = jnp.zeros((xpad_s.shape[0] - lat0 - n, cdim), BF16)
    dtv_s[0:nc, :] = dtc_ref[0].astype(F32)
    dtv_s[nc:nc + n, :] = dt_ref[0].astype(F32)

    lane = lax.broadcasted_iota(jnp.int32, (1, LANE), 1)
    a_row = -jnp.exp(alog_ref[...]) * LOG2E
    tri = tri_ref[...]
    shifts = shift_ref[...]
    half = SSM_CONV // 2
    side_taps = [k for k in range(SSM_CONV) if k != half]

    def prep_body(c, carry):
        rb = pl.multiple_of(c * q + jnp.where(c >= ncc, pad, 0), pad)
        rows = pl.ds(pl.multiple_of(c * q, q), q)
        for s0 in range(0, cdim, CONV_STRIP):
            strip = slice(s0, s0 + CONV_STRIP)
            taps = _dotf(shifts, xpad_s[pl.ds(rb, CONV_WIN), strip])
            centre = xpad_s[pl.ds(pl.multiple_of(rb + pad, pad), q), strip].astype(F32)
            acc = cb_ref[:, strip] + cw_ref[half:half + 1, strip] * centre
            for j, k in enumerate(side_taps):
                acc = acc + cw_ref[k:k + 1, strip] * taps[j * q:(j + 1) * q]
            u_s[rows, strip] = _silu(acc)
        raw = dtv_s[rows, :] + dtb_ref[...]
        dt = jnp.maximum(raw, 0.0) + jnp.log(1.0 + jnp.exp(-jnp.abs(raw)))
        dt = jnp.where(lane < 2 * SSM_HEADS, dt, 0.0)
        la = dt * a_row
        h3 = _split3(la)
        cs = _dotf(tri, h3[0]) + _dotf(tri, h3[1]) + _dotf(tri, h3[2])
        tot = cs[q - 1:q, :]
        g = jnp.where(lane < SSM_HEADS, cs, tot - cs + la)
        g_s[rows, :] = g
        live = lane < 2 * SSM_HEADS
        glog = jnp.where(live, g - jnp.log2(dt), 0.0)
        gt_s[c] = glog.T[0:2 * SSM_HEADS]
        wt_s[c] = jnp.where(live, jnp.exp2(tot - g) * dt, 0.0).T[0:2 * SSM_HEADS]
        for grp in range(SSM_GROUPS):
            bmt_s[c, grp] = u_s[rows, SSM_D_INNER + grp * SSM_STATE:SSM_D_INNER + (grp + 1) * SSM_STATE].T
        return carry

    lax.fori_loop(0, nch, prep_body, 0, unroll=2)

    ii = lax.broadcasted_iota(jnp.int32, (q, q), 0)
    jj = lax.broadcasted_iota(jnp.int32, (q, q), 1)
    hpg = SSM_HEADS // SSM_GROUPS
    gw = hpg * SSM_HEAD_DIM
    hl = lax.broadcasted_iota(jnp.int32, (1, gw), 1) // SSM_HEAD_DIM
    emask = [hl == e for e in range(hpg)]

    def chunk_update(c, d):
        rows = pl.ds(pl.multiple_of(c * q, q), q)
        g = g_s[rows, :]
        tot = jnp.where(lane < SSM_HEADS, g[q - 1:q, :], g[0:1, :])
        etot = jnp.exp2(tot)
        gl_t = gt_s[c]
        w_t = wt_s[c]
        keep = (ii >= jj) if d == 0 else (jj >= ii)
        ys = []
        for grp in range(SSM_GROUPS):
            bm = u_s[rows, SSM_D_INNER + grp * SSM_STATE:SSM_D_INNER + (grp + 1) * SSM_STATE]
            cm = u_s[rows, SSM_D_INNER + (SSM_GROUPS + grp) * SSM_STATE:
                     SSM_D_INNER + (SSM_GROUPS + grp + 1) * SSM_STATE]
            xs = u_s[rows, grp * gw:(grp + 1) * gw].astype(BF16)
            prev = st_s[:, grp * gw:(grp + 1) * gw]
            prev_b = prev.astype(BF16)
            cb = _dot_nt(cm.astype(BF16), bm.astype(BF16))
            bm_t = bmt_s[c, grp]
            zero = jnp.zeros_like(xs)
            lhs_y, rhs_y, lhs_s, rhs_s = [], [], [], []
            dec = jnp.zeros((1, gw), F32)
            for e in range(hpg):
                li = d * SSM_HEADS + grp * hpg + e
                gcol = jnp.broadcast_to(g[:, li:li + 1], (q, q))
                grow = jnp.broadcast_to(gl_t[li:li + 1, :], (q, q))
                mm = cb * jnp.exp2(jnp.where(keep, gcol - grow, -jnp.inf))
                l2 = jnp.exp2(gcol) * cm
                bw = bm_t * jnp.broadcast_to(w_t[li:li + 1, :], (q, q))
                xm = jnp.where(emask[e], xs, zero)
                pm = jnp.where(emask[e], prev_b, zero)
                lhs_y += [mm.astype(BF16), l2.astype(BF16)]
                rhs_y += [xm, pm]
                lhs_s.append(bw.astype(BF16))
                rhs_s.append(xm)
                dec = dec + jnp.where(emask[e], jnp.broadcast_to(etot[:, li:li + 1], (1, gw)), 0.0)
            ys.append(_dotf(jnp.concatenate(lhs_y, axis=-1), jnp.concatenate(rhs_y, axis=0)))
            sg = _dotf(jnp.concatenate(lhs_s, axis=-1), jnp.concatenate(rhs_s, axis=0))
            st_s[:, grp * gw:(grp + 1) * gw] = prev * dec + sg
        return jnp.concatenate(ys, axis=-1)

    st_s[...] = jnp.zeros_like(st_s)

    def fwd_body(c, carry):
        y_s[pl.ds(pl.multiple_of(c * q, q), q), :] = chunk_update(c, 0)
        return carry

    lax.fori_loop(0, nch, fwd_body, 0, unroll=2)

    st_s[...] = jnp.zeros_like(st_s)

    def finish(c, yb, zz):
        rows = pl.ds(pl.multiple_of(c * q, q), q)
        y = y_s[rows, :] + yb + u_s[rows, 0:SSM_D_INNER] * dsk_ref[...]
        y = y * _silu(zz.astype(F32))
        return _rms_rows(y, ng_ref[...]).astype(BF16)

    def bwd_ctx_body(s, carry):
        c = ncc - 1 - s
        yb = chunk_update(c, 1)
        if ctx_out:
            rows = pl.ds(pl.multiple_of(c * q, q), q)
            oc_ref[0, rows, :] = finish(c, yb, zc_ref[0, rows, :])
        return carry

    lax.fori_loop(0, ncc, bwd_ctx_body, 0)

    def bwd_lat_body(s, carry):
        c = nch - 1 - s
        yb = chunk_update(c, 1)
        rows = pl.ds(pl.multiple_of((c - ncc) * q, q), q)
        o_ref[0, rows, :] = finish(c, yb, z_ref[0, rows, :])
        return carry

    lax.fori_loop(0, nch - ncc, bwd_lat_body, 0, unroll=2)


def _ssd(p, pc, conv_w, conv_b, dt_bias, a_log, d_skip, norm_gain, ctx_out):
    bsz, n, _ = p.shape
    nc = pc.shape[1]
    di = SSM_D_INNER
    q = SSM_CHUNK
    padl = lambda v: jnp.pad(v.astype(F32).reshape(1, -1), ((0, 0), (0, LANE - 2 * SSM_HEADS)))
    dtb = padl(dt_bias)
    alog = padl(a_log)
    dsk = jnp.repeat(d_skip.astype(F32), SSM_HEAD_DIM).reshape(1, di)
    ng = norm_gain.astype(F32).reshape(1, di)
    tri = jnp.asarray(np.tril(np.ones((q, q))), BF16)
    z_blk = (3 * NA_WIDTH + 3 * DF_WIDTH) // di
    x_blk = (3 * NA_WIDTH + 3 * DF_WIDTH + di) // SSM_CONV_DIM
    spec = lambda nrow, w, j: pl.BlockSpec((1, nrow, w), lambda b: (b, 0, j))
    const = lambda shape: pl.BlockSpec(shape, lambda b: (0,) * len(shape))
    in_specs = [spec(n, di, z_blk), spec(n, SSM_CONV_DIM, x_blk), spec(n, LANE, DT_BLOCK)]
    args = [p, p, p]
    if ctx_out:
        in_specs.append(spec(nc, di, z_blk))
        args.append(pc)
    in_specs += [spec(nc, SSM_CONV_DIM, x_blk), spec(nc, LANE, DT_BLOCK),
                 const((SSM_CONV, SSM_CONV_DIM)), const((1, SSM_CONV_DIM)), const((1, LANE)), const((1, LANE)),
                 const((1, di)), const((1, di)), const((q, q)), const(((SSM_CONV - 1) * q, CONV_WIN))]
    side = [k for k in range(SSM_CONV) if k != SSM_CONV // 2]
    sel = np.zeros((len(side) * q, CONV_WIN), np.float32)
    for j, k in enumerate(side):
        sel[j * q + np.arange(q), CONV_PAD + np.arange(q) + k - SSM_CONV // 2] = 1.0
    args += [pc, pc, conv_w.astype(F32), conv_b.astype(F32).reshape(1, -1), dtb, alog, dsk, ng, tri,
             jnp.asarray(sel, BF16)]
    out_specs = [pl.BlockSpec((1, n, di), lambda b: (b, 0, 0))]
    out_shape = [jax.ShapeDtypeStruct((bsz, n, di), BF16)]
    if ctx_out:
        out_specs.append(pl.BlockSpec((1, nc, di), lambda b: (b, 0, 0)))
        out_shape.append(jax.ShapeDtypeStruct((bsz, nc, di), BF16))
    tot = n + nc
    xpad_rows = tot - q + CONV_PAD + CONV_WIN
    scratch = [pltpu.VMEM((xpad_rows, SSM_CONV_DIM), BF16), pltpu.VMEM((tot, SSM_CONV_DIM), F32),
               pltpu.VMEM((tot, LANE), F32), pltpu.VMEM((tot, LANE), F32), pltpu.VMEM((tot, di), F32),
               pltpu.VMEM((SSM_STATE, di), F32),
               pltpu.VMEM((tot // q, 2 * SSM_HEADS, q), F32), pltpu.VMEM((tot // q, 2 * SSM_HEADS, q), F32),
               pltpu.VMEM((tot // q, SSM_GROUPS, SSM_STATE, q), F32)]
    outs = pl.pallas_call(
        functools.partial(_ssd_kernel, ctx_out=ctx_out),
        grid=(bsz,),
        in_specs=in_specs,
        out_specs=out_specs,
        out_shape=out_shape,
        scratch_shapes=scratch,
        compiler_params=_cparams(("parallel",)),
        name="ssd",
    )(*args)
    return (outs[0], outs[1]) if ctx_out else (outs[0], None)


def _mlp_kernel(x_ref, xp_ref, xn_ref, a_ref, ap_ref, an_ref, b_ref, bp_ref, bn_ref, c_ref, cp_ref, cn_ref,
                g1_ref, sh_ref, sc_ref, g2_ref, gn_ref, wo_ref, wu_ref, cw_ref, cb_ref, wd_ref, o_ref, *, n_tiles):
    i = pl.program_id(1)
    tm = x_ref.shape[1]
    halo = xp_ref.shape[1]
    ext = tm + 2 * halo
    y_ext = jnp.concatenate([
        jnp.concatenate([ap_ref[0], bp_ref[0], cp_ref[0]], axis=-1),
        jnp.concatenate([a_ref[0], b_ref[0], c_ref[0]], axis=-1),
        jnp.concatenate([an_ref[0], bn_ref[0], cn_ref[0]], axis=-1)], axis=0)
    x_ext = jnp.concatenate([xp_ref[0], x_ref[0], xn_ref[0]], axis=0)
    x1_ext = x_ext + g1_ref[0] * _dotf(y_ext, wo_ref[...])
    h = _rms_rows(x1_ext, gn_ref[...]) * (1.0 + sc_ref[0]) + sh_ref[0]
    r = lax.broadcasted_iota(jnp.int32, (ext, 1), 0)
    keep = jnp.where(r < halo, jnp.where(i > 0, 1.0, 0.0),
                     jnp.where(r >= halo + tm, jnp.where(i < n_tiles - 1, 1.0, 0.0), 1.0))
    h_ext = (h * keep).astype(BF16)
    h_mid = h_ext[halo:halo + tm]
    x1 = x1_ext[halo:halo + tm]
    acc = jnp.zeros((tm, D_MODEL), F32)
    lo = 0
    for width in FFN_CHUNKS:
        cols = slice(lo, lo + width)
        gate = _dotf(h_ext, wu_ref[:, cols])
        val = _dotf(h_mid, wu_ref[:, D_FF + lo:D_FF + lo + width])
        conv = (cb_ref[:, cols]
                + cw_ref[0:1, cols] * pltpu.roll(gate, 1, 0)[halo:halo + tm]
                + cw_ref[1:2, cols] * gate[halo:halo + tm]
                + cw_ref[2:3, cols] * pltpu.roll(gate, ext - 1, 0)[halo:halo + tm])
        act = (_silu(conv) * val).astype(BF16)
        acc = acc + _dotf(act, wd_ref[cols, :])
        lo += width
    o_ref[0] = x1 + g2_ref[0] * acc


def _mlp(x, y_na, y_df, y_ssm, mod, gain, w_out, w_up, conv_w, conv_b, w_down, fixed_row):
    bsz, n, _ = x.shape
    tm = min(FFN_TM, n)
    n_tiles = n // tm
    halo = MLP_HALO
    hb = tm // halo
    last = n // halo - 1
    const = lambda shape: pl.BlockSpec(shape, lambda b, i: (0,) * len(shape), pipeline_mode=pl.Buffered(1))

    def rows3(width):
        return [pl.BlockSpec((1, tm, width), lambda b, i: (b, i, 0)),
                pl.BlockSpec((1, halo, width), lambda b, i: (b, jnp.maximum(i * hb - 1, 0), 0)),
                pl.BlockSpec((1, halo, width), lambda b, i: (b, jnp.minimum((i + 1) * hb, last), 0))]

    return pl.pallas_call(
        functools.partial(_mlp_kernel, n_tiles=n_tiles),
        grid=(bsz, n_tiles),
        in_specs=rows3(D_MODEL) + rows3(NA_WIDTH) + rows3(DF_WIDTH) + rows3(SSM_D_INNER) + [
            _mod_spec(2, fixed_row),
            _mod_spec(3, fixed_row),
            _mod_spec(4, fixed_row),
            _mod_spec(5, fixed_row),
            pl.BlockSpec((1, D_MODEL), lambda b, i: (0, 0)),
            const((D_MIX, D_MODEL)),
            const((D_MODEL, 2 * D_FF)),
            pl.BlockSpec((FFN_CONV, D_FF), lambda b, i: (0, 0)),
            pl.BlockSpec((1, D_FF), lambda b, i: (0, 0)),
            const((D_FF, D_MODEL)),
        ],
        out_specs=pl.BlockSpec((1, tm, D_MODEL), lambda b, i: (b, i, 0)),
        out_shape=jax.ShapeDtypeStruct((bsz, n, D_MODEL), F32),
        compiler_params=_cparams(("parallel", "parallel")),
        name="mlp",
    )(x, x, x, y_na, y_na, y_na, y_df, y_df, y_df, y_ssm, y_ssm, y_ssm, mod, mod, mod, mod, gain,
      w_out, w_up, conv_w, conv_b, w_down)


def kernel(x, c, ctx, c_ctx, w_ada, b_ada, g_mix, g_ffn, w_in, na_q_gain, na_k_gain, na_rpb, df_q_gain, df_k_gain, df_lambda, df_subln, ssm_conv_w, ssm_conv_b, ssm_dt_bias, ssm_a_log, ssm_d, ssm_norm, w_out, ffn_w_up, ffn_conv_w, ffn_conv_b, ffn_w_down):
    bsz, n, _ = x.shape
    ctx_row = bsz
    mod_rows = -(-(bsz + 1) // SUBLANE) * SUBLANE
    cc = jnp.concatenate([c, c_ctx[None, :], jnp.zeros((mod_rows - bsz - 1, D_MODEL), c.dtype)], axis=0)
    mods = _ada(cc.astype(F32), w_ada.astype(F32), b_ada.astype(F32))
    mods = mods.reshape(DEPTH, mod_rows, 1, 6 * D_MODEL)
    tables = _rope_tables(n)
    w_in_b = jnp.pad(w_in, ((0, 0), (0, 0), (0, IN_COLS_PAD - IN_COLS))).astype(BF16)
    w_out_b = w_out.astype(BF16)
    w_up_b = ffn_w_up.astype(BF16)
    w_down_b = ffn_w_down.astype(BF16)
    x = x.astype(F32)
    ctx = ctx.astype(F32)
    for l in range(DEPTH):
        ctx_out = l < DEPTH - 1
        mod = mods[l]
        gm = g_mix[l].astype(F32).reshape(1, D_MODEL)
        gf = g_ffn[l].astype(F32).reshape(1, D_MODEL)
        p = _inproj(x, mod, gm, w_in_b[l], None)
        pc = _inproj(ctx, mod, gm, w_in_b[l], ctx_row)
        y_na, yc_na = _na(p, pc, na_q_gain[l], na_k_gain[l], na_rpb[l], ctx_out)
        y_df, yc_df = _df(p, pc, tables, df_q_gain[l], df_k_gain[l], df_lambda[l], df_subln[l], l, ctx_out)
        y_ssm, yc_ssm = _ssd(p, pc, ssm_conv_w[l], ssm_conv_b[l], ssm_dt_bias[l], ssm_a_log[l], ssm_d[l],
                             ssm_norm[l], ctx_out)
        cw = ffn_conv_w[l].astype(F32)
        cb = ffn_conv_b[l].astype(F32).reshape(1, D_FF)
        x = _mlp(x, y_na, y_df, y_ssm, mod, gf, w_out_b[l], w_up_b[l], cw, cb, w_down_b[l], None)
        if ctx_out:
            ctx = _mlp(ctx, yc_na, yc_df, yc_ssm, mod, gf, w_out_b[l], w_up_b[l], cw, cb, w_down_b[l], ctx_row)
    return x
```

```python
import functools
import math

import numpy as np
import jax
import jax.numpy as jnp
from jax import lax
from jax.experimental import pallas as pl
from jax.experimental.pallas import tpu as pltpu

F32 = jnp.float32
BF16 = jnp.bfloat16

D_MODEL = 1024
DEPTH = 2
GRID_W = 64
NORM_EPS = 1e-6
NA_HEADS = 4
NA_HEAD_DIM = 64
NA_ROWS = 8
NA_COLS = 16
NA_WIDTH = NA_HEADS * NA_HEAD_DIM
DF_HEADS = 4
DF_HEAD_DIM = 32
DF_WIDTH = DF_HEADS * 2 * DF_HEAD_DIM
ROPE_BASE = 10000.0
SSM_HEADS = 8
SSM_HEAD_DIM = 64
SSM_D_INNER = SSM_HEADS * SSM_HEAD_DIM
SSM_GROUPS = 2
SSM_STATE = 128
SSM_CONV = 5
SSM_CHUNK = 128
SSM_CONV_DIM = SSM_D_INNER + 2 * SSM_GROUPS * SSM_STATE
D_MIX = NA_WIDTH + DF_WIDTH + SSM_D_INNER
IN_COLS = 3 * NA_WIDTH + 3 * DF_WIDTH + SSM_D_INNER + SSM_CONV_DIM + 2 * SSM_HEADS
D_FF = 2816
FFN_CONV = 3

LANE = 128
SUBLANE = 8
IN_COLS_PAD = -(-IN_COLS // LANE) * LANE
DT_BLOCK = IN_COLS_PAD // LANE - 1
VMEM_LIMIT = 56 * 1024 * 1024

IN_TM = 1024
FFN_TM = 1024
FFN_CHUNKS = (1536, 1280)
MLP_HALO = 16
DF_TQ = 128
DF_UNROLL = 4
DF_KBLOCK = 768
DF_KCHUNK = 128
DF_VT_ROWS =2 * DF_HEAD_DIM + 16
CONV_STRIP = 2 * LANE
CONV_PAD = 16
CONV_WIN = 2 * SSM_CHUNK
NA_UNROLL = 16
LOG2E = math.log2(math.e)


def _cparams(sem):
    return pltpu.CompilerParams(dimension_semantics=sem, vmem_limit_bytes=VMEM_LIMIT)


def _dotf(a, b):
    return jnp.dot(a, b, preferred_element_type=F32)


def _dot_nt(a, b):
    return lax.dot_general(a, b, (((1,), (1,)), ((), ())), preferred_element_type=F32)


def _split2(x):
    hi = x.astype(BF16)
    lo = (x - hi.astype(F32)).astype(BF16)
    return hi, lo


def _split3(x):
    hi = x.astype(BF16)
    r = x - hi.astype(F32)
    mid = r.astype(BF16)
    lo = (r - mid.astype(F32)).astype(BF16)
    return hi, mid, lo


def _segmean(xsq, bd):
    hi, lo = _split2(xsq)
    return _dotf(hi, bd) + _dotf(lo, bd)


def _sigmoid(x):
    return 1.0 / (1.0 + jnp.exp(-x))


def _silu(x):
    return x * _sigmoid(x)


def _rms_rows(xf, gain):
    ms = jnp.mean(xf * xf, axis=-1, keepdims=True)
    return xf * lax.rsqrt(ms + NORM_EPS) * gain


def _ada_kernel(c_ref, w_ref, b_ref, o_ref):
    s = _silu(c_ref[...])
    shi, slo = _split2(s)
    whi, wlo = _split2(w_ref[0])
    o_ref[0] = _dotf(shi, whi) + _dotf(shi, wlo) + _dotf(slo, whi) + b_ref[0]


def _ada(cc, w_ada, b_ada):
    rows = cc.shape[0]
    nblk = 6
    return pl.pallas_call(
        _ada_kernel,
        grid=(DEPTH, nblk),
        in_specs=[
            pl.BlockSpec((rows, D_MODEL), lambda l, j: (0, 0)),
            pl.BlockSpec((1, D_MODEL, D_MODEL), lambda l, j: (l, 0, j)),
            pl.BlockSpec((1, 1, D_MODEL), lambda l, j: (l, 0, j)),
        ],
        out_specs=pl.BlockSpec((1, rows, D_MODEL), lambda l, j: (l, 0, j)),
        out_shape=jax.ShapeDtypeStruct((DEPTH, rows, 6 * D_MODEL), F32),
        compiler_params=_cparams(("parallel", "parallel")),
        name="adaln",
    )(cc, w_ada, b_ada.reshape(DEPTH, 1, 6 * D_MODEL))


def _mod_spec(chunk, fixed_row):
    if fixed_row is None:
        return pl.BlockSpec((1, 1, D_MODEL), lambda b, i: (b, 0, chunk))
    return pl.BlockSpec((1, 1, D_MODEL), lambda b, i: (fixed_row, 0, chunk))


def _inproj_kernel(x_ref, sh_ref, sc_ref, g_ref, w_ref, o_ref):
    h = _rms_rows(x_ref[0], g_ref[...]) * (1.0 + sc_ref[0]) + sh_ref[0]
    o_ref[0] = _dotf(h.astype(BF16), w_ref[0]).astype(BF16)


def _inproj(x, mod, gain, w, layer, fixed_row):
    bsz, n, _ = x.shape
    tm = min(IN_TM, n)
    return pl.pallas_call(
        _inproj_kernel,
        grid=(bsz, n // tm),
        in_specs=[
            pl.BlockSpec((1, tm, D_MODEL), lambda b, i: (b, i, 0)),
            _mod_spec(0, fixed_row),
            _mod_spec(1, fixed_row),
            pl.BlockSpec((1, D_MODEL), lambda b, i: (0, 0)),
            pl.BlockSpec((1, D_MODEL, IN_COLS_PAD), lambda b, i: (layer, 0, 0)),
        ],
        out_specs=pl.BlockSpec((1, tm, IN_COLS_PAD), lambda b, i: (b, i, 0)),
        out_shape=jax.ShapeDtypeStruct((bsz, n, IN_COLS_PAD), BF16),
        compiler_params=_cparams(("parallel", "parallel")),
        name="inproj",
    )(x, mod, mod, gain, w)


def _na_kernel(*refs, rows, ctx_out):
    if ctx_out:
        (q_ref, k_ref, v_ref, qc_ref, kc_ref, vc_ref, qg_ref, kg_ref, bias_ref, bd_ref,
         o_ref, oc_ref, qn_s, kn_s, kcn_s, qcn_s) = refs
    else:
        (q_ref, k_ref, v_ref, kc_ref, vc_ref, qg_ref, kg_ref, bias_ref, bd_ref,
         o_ref, qn_s, kn_s, kcn_s) = refs
    bd = bd_ref[...]
    scale = NA_HEAD_DIM ** -0.5 * LOG2E

    def norm(x, g):
        xf = x.astype(F32)
        return xf * lax.rsqrt(_segmean(xf * xf, bd) + NORM_EPS) * g

    qn_s[...] = (norm(q_ref[0], qg_ref[...]) * scale).astype(BF16)
    kn_s[...] = norm(k_ref[0], kg_ref[...]).astype(BF16)
    kcn_s[...] = norm(kc_ref[0], kg_ref[...]).astype(BF16)

    head = lax.broadcasted_iota(jnp.int32, (1, NA_WIDTH), 1) // NA_HEAD_DIM
    hmask = [head == h for h in range(NA_HEADS)]
    wr = min(NA_ROWS, rows)
    nloc = wr * GRID_W

    def stack_heads(q):
        return jnp.concatenate([jnp.where(hmask[h], q, jnp.zeros_like(q)) for h in range(NA_HEADS)], axis=0)

    def unstack_heads(o):
        out = jnp.where(hmask[0], o[0:GRID_W], 0.0)
        for h in range(1, NA_HEADS):
            out = out + jnp.where(hmask[h], o[h * GRID_W:(h + 1) * GRID_W], 0.0)
        return out

    def row_body(r, carry):
        kr0 = jnp.clip(r - wr // 2, 0, rows - wr)
        d0 = kr0 - r + NA_ROWS - 1
        q4 = stack_heads(qn_s[pl.ds(pl.multiple_of(r * GRID_W, GRID_W), GRID_W), :])
        kstart = pl.multiple_of(kr0 * GRID_W, GRID_W)
        s_loc = _dot_nt(q4, kn_s[pl.ds(kstart, nloc), :]) + bias_ref[d0]
        s_ctx = _dot_nt(q4, kcn_s[...])
        m = jnp.maximum(jnp.max(s_loc, axis=-1, keepdims=True), jnp.max(s_ctx, axis=-1, keepdims=True))
        p_loc = jnp.exp2(s_loc - m)
        p_ctx = jnp.exp2(s_ctx - m)
        l = jnp.sum(p_loc, axis=-1, keepdims=True) + jnp.sum(p_ctx, axis=-1, keepdims=True)
        o = _dotf(p_loc.astype(BF16), v_ref[0, pl.ds(kstart, nloc), :]) + _dotf(p_ctx.astype(BF16), vc_ref[0])
        o = o * (1.0 / l)
        o_ref[0, pl.ds(pl.multiple_of(r * GRID_W, GRID_W), GRID_W), :] = unstack_heads(o).astype(BF16)
        return carry

    lax.fori_loop(0, rows, row_body, 0, unroll=NA_UNROLL)

    if ctx_out:
        qcn_s[...] = (norm(qc_ref[0], qg_ref[...]) * scale).astype(BF16)
        nblk = qcn_s.shape[0] // GRID_W

        def ctx_body(r, carry):
            rs = pl.multiple_of(r * GRID_W, GRID_W)
            q4 = stack_heads(qcn_s[pl.ds(rs, GRID_W), :])
            s = _dot_nt(q4, kcn_s[...])
            m = jnp.max(s, axis=-1, keepdims=True)
            p = jnp.exp2(s - m)
            l = jnp.sum(p, axis=-1, keepdims=True)
            o = _dotf(p.astype(BF16), vc_ref[0]) * (1.0 / l)
            oc_ref[0, pl.ds(rs, GRID_W), :] = unstack_heads(o).astype(BF16)
            return carry

        lax.fori_loop(0, nblk, ctx_body, 0)


def _na_bias_table(rpb, rows):
    wr = min(NA_ROWS, rows)
    qc = np.arange(GRID_W)
    kc = np.arange(GRID_W)
    dc = np.clip(kc[None, :] - qc[:, None], 1 - NA_COLS, NA_COLS - 1) + NA_COLS - 1
    ws = np.clip(qc - NA_COLS // 2, 0, GRID_W - NA_COLS)
    ok = (kc[None, :] >= ws[:, None]) & (kc[None, :] < ws[:, None] + NA_COLS)
    onehot = jnp.asarray(dc[None, :, :] == np.arange(2 * NA_COLS - 1)[:, None, None], F32)
    t = jnp.einsum('hrc,cqk->hrqk', rpb.astype(F32), onehot, precision=lax.Precision.HIGHEST)
    t = jnp.where(ok[None, None], t, -jnp.inf)
    t = jnp.stack([t[:, d0:d0 + wr] for d0 in range(NA_ROWS)], axis=0)
    return t.transpose(0, 1, 3, 2, 4).reshape(NA_ROWS, NA_HEADS * GRID_W, wr * GRID_W) * LOG2E


def _na(p, pc, q_gain, k_gain, rpb, ctx_out):
    bsz, n, _ = p.shape
    nc = pc.shape[1]
    rows = n // GRID_W
    w = NA_WIDTH
    bias = _na_bias_table(rpb, rows)
    bd = jnp.asarray(np.kron(np.eye(NA_HEADS), np.full((NA_HEAD_DIM, NA_HEAD_DIM), 1.0 / NA_HEAD_DIM)), BF16)
    qg = jnp.tile(q_gain.astype(F32), NA_HEADS).reshape(1, w)
    kg = jnp.tile(k_gain.astype(F32), NA_HEADS).reshape(1, w)
    col = lambda nrow, j: pl.BlockSpec((1, nrow, w), lambda b: (b, 0, j))
    const = lambda shape: pl.BlockSpec(shape, lambda b: (0,) * len(shape))
    in_specs = [col(n, 0), col(n, 1), col(n, 2)]
    args = [p, p, p]
    if ctx_out:
        in_specs.append(col(nc, 0))
        args.append(pc)
    in_specs += [col(nc, 1), col(nc, 2), const((1, w)), const((1, w)), const(bias.shape), const((w, w))]
    args += [pc, pc, qg, kg, bias, bd]
    out_specs = [pl.BlockSpec((1, n, w), lambda b: (b, 0, 0))]
    out_shape = [jax.ShapeDtypeStruct((bsz, n, w), BF16)]
    scratch = [pltpu.VMEM((n, w), BF16), pltpu.VMEM((n, w), BF16), pltpu.VMEM((nc, w), BF16)]
    if ctx_out:
        out_specs.append(pl.BlockSpec((1, nc, w), lambda b: (b, 0, 0)))
        out_shape.append(jax.ShapeDtypeStruct((bsz, nc, w), BF16))
        scratch.append(pltpu.VMEM((nc, w), BF16))
    outs = pl.pallas_call(
        functools.partial(_na_kernel, rows=rows, ctx_out=ctx_out),
        grid=(bsz,),
        in_specs=in_specs,
        out_specs=out_specs,
        out_shape=out_shape,
        scratch_shapes=scratch,
        compiler_params=_cparams(("parallel",)),
        name="na_attn",
    )(*args)
    return (outs[0], outs[1]) if ctx_out else (outs[0], None)


def _rope_tables(n):
    per_axis = DF_HEAD_DIM // 2
    inv_freq = ROPE_BASE ** (-jnp.arange(0, per_axis, 2, dtype=F32) / per_axis)
    t = jnp.arange(n, dtype=jnp.int32)
    pos = jnp.stack([t // GRID_W, t % GRID_W], axis=-1).astype(F32)
    ang = pos[:, :, None] * inv_freq
    ang = jnp.concatenate([ang, ang], axis=-1).reshape(n, DF_HEAD_DIM)
    reps = DF_WIDTH // DF_HEAD_DIM
    cos = jnp.tile(jnp.cos(ang), (1, reps))
    sin = jnp.tile(jnp.sin(ang), (1, reps))
    first = (np.arange(DF_WIDTH) % (DF_HEAD_DIM // 2)) < DF_HEAD_DIM // 4
    sin_a = jnp.where(first[None, :], -sin, 0.0)
    sin_b = jnp.where(first[None, :], 0.0, sin)
    return cos, sin_a, sin_b


def _df_kernel(*refs, ctx_out, lam_init, tq):
    if ctx_out:
        (q_ref, k_ref, v_ref, qc_ref, kc_ref, vc_ref, cos_ref, sa_ref, sb_ref, qg_ref, kg_ref, sub_ref,
         lam_ref, bd32_ref, bd64_ref, o_ref, oc_ref, qn_s, kall_s, vt_s, qcn_s) = refs
    else:
        (q_ref, k_ref, v_ref, kc_ref, vc_ref, cos_ref, sa_ref, sb_ref, qg_ref, kg_ref, sub_ref,
         lam_ref, bd32_ref, bd64_ref, o_ref, qn_s, kall_s, vt_s) = refs
    n = q_ref.shape[1]
    nc = kc_ref.shape[1]
    bd32 = bd32_ref[...]
    bd64 = bd64_ref[...]
    scale = DF_HEAD_DIM ** -0.5 * LOG2E
    lam = lam_ref[0, 0]
    shift = DF_HEAD_DIM // 4

    def norm(x, g):
        xf = x.astype(F32)
        return xf * lax.rsqrt(_segmean(xf * xf, bd32) + NORM_EPS) * g

    def rope(xn):
        return (xn * cos_ref[...] + pltpu.roll(xn, DF_WIDTH - shift, 1) * sa_ref[...]
                + pltpu.roll(xn, shift, 1) * sb_ref[...])

    qn_s[...] = (rope(norm(q_ref[0], qg_ref[...])) * scale).T.astype(BF16)
    kall_s[0:n, :] = rope(norm(k_ref[0], kg_ref[...])).astype(BF16)
    kall_s[n:n + nc, :] = norm(kc_ref[0], kg_ref[...]).astype(BF16)
    hd = 2 * DF_HEAD_DIM
    tail_row = lax.broadcasted_iota(jnp.int32, (DF_VT_ROWS - hd, n + nc), 0)
    tail = jnp.where(tail_row == 0, 1.0, 0.0).astype(BF16)
    v_t = v_ref[0].astype(F32).T
    vc_t = vc_ref[0].astype(F32).T
    for h in range(DF_HEADS):
        vt_s[h, 0:hd, 0:n] = v_t[h * hd:(h + 1) * hd].astype(BF16)
        vt_s[h, 0:hd, n:n + nc] = vc_t[h * hd:(h + 1) * hd].astype(BF16)
        vt_s[h, hd:DF_VT_ROWS, :] = tail

    comp = lax.broadcasted_iota(jnp.int32, (DF_WIDTH, 1), 0) // DF_HEAD_DIM
    sub_gain = sub_ref[...] * (1.0 - lam_init)

    def attend(q, k_lo, k_hi):
        t = q.shape[1]
        zero = jnp.zeros_like(q)
        nk = k_hi - k_lo
        blk = min(DF_KBLOCK, nk)

        def scores(h):
            q2 = jnp.concatenate([jnp.where(comp == 2 * h, q, zero), jnp.where(comp == 2 * h + 1, q, zero)], axis=1)
            return [_dotf(kall_s[k_lo + b0:k_lo + b0 + blk, :], q2) for b0 in range(0, nk, blk)]

        parts = []
        s_next = scores(0)
        for h in range(DF_HEADS):
            s_blocks = s_next
            if h + 1 < DF_HEADS:
                s_next = scores(h + 1)
            m = acc = None
            for bi, s_blk in enumerate(s_blocks):
                for c0 in range(0, blk, DF_KCHUNK):
                    s_c = s_blk[c0:c0 + DF_KCHUNK]
                    k0 = k_lo + bi * blk + c0
                    m_c = jnp.max(s_c, axis=0, keepdims=True)
                    m_new = m_c if m is None else jnp.maximum(m, m_c)
                    p_c = jnp.exp2((s_c - m_new).astype(BF16))
                    o_c = _dotf(vt_s[h, :, k0:k0 + DF_KCHUNK], p_c)
                    acc = o_c if acc is None else acc * jnp.exp2(m - m_new) + o_c
                    m = m_new
            o = acc[0:hd] * (1.0 / acc[hd:hd + 1])
            parts.append(o[:, 0:t] - lam * o[:, t:2 * t])
        y = jnp.concatenate(parts, axis=0).T
        return y * lax.rsqrt(_segmean(y * y, bd64) + NORM_EPS) * sub_gain

    def q_body(i, carry):
        rs = pl.multiple_of(i * tq, tq)
        o_ref[0, pl.ds(rs, tq), :] = attend(qn_s[:, pl.ds(rs, tq)], 0, n + nc).astype(BF16)
        return carry

    lax.fori_loop(0, n // tq, q_body, 0, unroll=DF_UNROLL)

    if ctx_out:
        qcn_s[...] = (norm(qc_ref[0], qg_ref[...]) * scale).T.astype(BF16)
        tc = min(tq, nc)

        def c_body(i, carry):
            rs = pl.multiple_of(i * tc, tc)
            oc_ref[0, pl.ds(rs, tc), :] = attend(qcn_s[:, pl.ds(rs, tc)], n, n + nc).astype(BF16)
            return carry

        lax.fori_loop(0, nc // tc, c_body, 0)


def _df(p, pc, tables, q_gain, k_gain, lam_vecs, subln, layer, ctx_out):
    bsz, n, _ = p.shape
    nc = pc.shape[1]
    w = DF_WIDTH
    cos, sin_a, sin_b = tables
    lam_init = 0.8 - 0.6 * math.exp(-0.3 * layer)
    lv = lam_vecs.astype(F32)
    lam = (jnp.exp(jnp.sum(lv[0] * lv[1])) - jnp.exp(jnp.sum(lv[2] * lv[3])) + lam_init).reshape(1, 1)
    bd32 = jnp.asarray(np.kron(np.eye(w // DF_HEAD_DIM), np.full((DF_HEAD_DIM,) * 2, 1.0 / DF_HEAD_DIM)), BF16)
    bd64 = jnp.asarray(np.kron(np.eye(DF_HEADS), np.full((2 * DF_HEAD_DIM,) * 2, 0.5 / DF_HEAD_DIM)), BF16)
    qg = jnp.tile(q_gain.astype(F32), w // DF_HEAD_DIM).reshape(1, w)
    kg = jnp.tile(k_gain.astype(F32), w // DF_HEAD_DIM).reshape(1, w)
    sub = jnp.tile(subln.astype(F32), DF_HEADS).reshape(1, w)
    col = lambda nrow, j: pl.BlockSpec((1, nrow, w), lambda b: (b, 0, j))
    const = lambda shape: pl.BlockSpec(shape, lambda b: (0,) * len(shape))
    in_specs = [col(n, 3), col(n, 4), col(n, 5)]
    args = [p, p, p]
    if ctx_out:
        in_specs.append(col(nc, 3))
        args.append(pc)
    in_specs += [col(nc, 4), col(nc, 5), const((n, w)), const((n, w)), const((n, w)),
                 const((1, w)), const((1, w)), const((1, w)),
                 pl.BlockSpec(memory_space=pltpu.SMEM), const((w, w)), const((w, w))]
    args += [pc, pc, cos, sin_a, sin_b, qg, kg, sub, lam, bd32, bd64]
    out_specs = [pl.BlockSpec((1, n, w), lambda b: (b, 0, 0))]
    out_shape = [jax.ShapeDtypeStruct((bsz, n, w), BF16)]
    scratch = [pltpu.VMEM((w, n), BF16), pltpu.VMEM((n + nc, w), BF16),
               pltpu.VMEM((DF_HEADS, DF_VT_ROWS, n + nc), BF16)]
    if ctx_out:
        out_specs.append(pl.BlockSpec((1, nc, w), lambda b: (b, 0, 0)))
        out_shape.append(jax.ShapeDtypeStruct((bsz, nc, w), BF16))
        scratch.append(pltpu.VMEM((w, nc), BF16))
    outs = pl.pallas_call(
        functools.partial(_df_kernel, ctx_out=ctx_out, lam_init=lam_init, tq=DF_TQ),
        grid=(bsz,),
        in_specs=in_specs,
        out_specs=out_specs,
        out_shape=out_shape,
        scratch_shapes=scratch,
        compiler_params=_cparams(("parallel",)),
        name="diff_attn",
    )(*args)
    return (outs[0], outs[1]) if ctx_out else (outs[0], None)


def _ssd_kernel(*refs, ctx_out):
    if ctx_out:
        (z_ref, xbc_ref, dt_ref, zc_ref, xbcc_ref, dtc_ref, cw_ref, cb_ref, dtb_ref, alog_ref, dsk_ref, ng_ref,
         tri_ref, shift_ref, o_ref, oc_ref, xpad_s, u_s, g_s, dtv_s, y_s, st_s, gt_s, wt_s, bmt_s) = refs
    else:
        (z_ref, xbc_ref, dt_ref, xbcc_ref, dtc_ref, cw_ref, cb_ref, dtb_ref, alog_ref, dsk_ref, ng_ref,
         tri_ref, shift_ref, o_ref, xpad_s, u_s, g_s, dtv_s, y_s, st_s, gt_s, wt_s, bmt_s) = refs
        zc_ref = oc_ref = None
    n = xbc_ref.shape[1]
    nc = xbcc_ref.shape[1]
    q = SSM_CHUNK
    ncc = nc // q
    nch = (n + nc) // q
    pad = CONV_PAD
    cdim = SSM_CONV_DIM
    lat0 = 2 * pad + nc

    xpad_s[0:pad, :] = jnp.zeros((pad, cdim), BF16)
    xpad_s[pad:pad + nc, :] = xbcc_ref[0]
    xpad_s[pad + nc:lat0, :] = jnp.zeros((pad, cdim), BF16)
    xpad_s[lat0:lat0 + n, :] = xbc_ref[0]
    xpad_s[lat0 + n:, :] = jnp.zeros((xpad_s.shape[0] - lat0 - n, cdim), BF16)
    dtv_s[0:nc, :] = dtc_ref[0].astype(F32)
    dtv_s[nc:nc + n, :] = dt_ref[0].astype(F32)

    lane = lax.broadcasted_iota(jnp.int32, (1, LANE), 1)
    a_row = -jnp.exp(alog_ref[...]) * LOG2E
    tri = tri_ref[...]
    shifts = shift_ref[...]
    half = SSM_CONV // 2
    side_taps = [k for k in range(SSM_CONV) if k != half]

    def prep_body(c, carry):
        rb = pl.multiple_of(c * q + jnp.where(c >= ncc, pad, 0), pad)
        rows = pl.ds(pl.multiple_of(c * q, q), q)
        for s0 in range(0, cdim, CONV_STRIP):
            strip = slice(s0, s0 + CONV_STRIP)
            taps = _dotf(shifts, xpad_s[pl.ds(rb, CONV_WIN), strip])
            centre = xpad_s[pl.ds(pl.multiple_of(rb + pad, pad), q), strip].astype(F32)
            acc = cb_ref[:, strip] + cw_ref[half:half + 1, strip] * centre
            for j, k in enumerate(side_taps):
                acc = acc + cw_ref[k:k + 1, strip] * taps[j * q:(j + 1) * q]
            u_s[rows, strip] = _silu(acc)
        raw = dtv_s[rows, :] + dtb_ref[...]
        dt = jnp.maximum(raw, 0.0) + jnp.log(1.0 + jnp.exp(-jnp.abs(raw)))
        dt = jnp.where(lane < 2 * SSM_HEADS, dt, 0.0)
        la = dt * a_row
        h3 = _split3(la)
        cs = _dotf(tri, h3[0]) + _dotf(tri, h3[1]) + _dotf(tri, h3[2])
        tot = cs[q - 1:q, :]
        g = jnp.where(lane < SSM_HEADS, cs, tot - cs + la)
        g_s[rows, :] = g
        live = lane < 2 * SSM_HEADS
        glog = jnp.where(live, g - jnp.log2(dt), 0.0)
        gt_s[c] = glog.T[0:2 * SSM_HEADS]
        wt_s[c] = jnp.where(live, jnp.exp2(tot - g) * dt, 0.0).T[0:2 * SSM_HEADS]
        for grp in range(SSM_GROUPS):
            bmt_s[c, grp] = u_s[rows, SSM_D_INNER + grp * SSM_STATE:SSM_D_INNER + (grp + 1) * SSM_STATE].T
        return carry

    lax.fori_loop(0, nch, prep_body, 0, unroll=2)

    ii = lax.broadcasted_iota(jnp.int32, (q, q), 0)
    jj = lax.broadcasted_iota(jnp.int32, (q, q), 1)
    hpg = SSM_HEADS // SSM_GROUPS
    gw = hpg * SSM_HEAD_DIM
    hl = lax.broadcasted_iota(jnp.int32, (1, gw), 1) // SSM_HEAD_DIM
    emask = [hl == e for e in range(hpg)]

    def chunk_update(c, d):
        rows = pl.ds(pl.multiple_of(c * q, q), q)
        g = g_s[rows, :]
        tot = jnp.where(lane < SSM_HEADS, g[q - 1:q, :], g[0:1, :])
        etot = jnp.exp2(tot)
        gl_t = gt_s[c]
        w_t = wt_s[c]
        keep = (ii >= jj) if d == 0 else (jj >= ii)
        ys = []
        for grp in range(SSM_GROUPS):
            bm = u_s[rows, SSM_D_INNER + grp * SSM_STATE:SSM_D_INNER + (grp + 1) * SSM_STATE]
            cm = u_s[rows, SSM_D_INNER + (SSM_GROUPS + grp) * SSM_STATE:
                     SSM_D_INNER + (SSM_GROUPS + grp + 1) * SSM_STATE]
            xs = u_s[rows, grp * gw:(grp + 1) * gw].astype(BF16)
            prev = st_s[:, grp * gw:(grp + 1) * gw]
            prev_b = prev.astype(BF16)
            cb = _dot_nt(cm.astype(BF16), bm.astype(BF16))
            bm_t = bmt_s[c, grp]
            zero = jnp.zeros_like(xs)
            lhs_y, rhs_y, lhs_s, rhs_s = [], [], [], []
            dec = jnp.zeros((1, gw), F32)
            for e in range(hpg):
                li = d * SSM_HEADS + grp * hpg + e
                gcol = jnp.broadcast_to(g[:, li:li + 1], (q, q))
                grow = jnp.broadcast_to(gl_t[li:li + 1, :], (q, q))
                mm = cb * jnp.exp2(jnp.where(keep, gcol - grow, -jnp.inf))
                l2 = jnp.exp2(gcol) * cm
                bw = bm_t * jnp.broadcast_to(w_t[li:li + 1, :], (q, q))
                xm = jnp.where(emask[e], xs, zero)
                pm = jnp.where(emask[e], prev_b, zero)
                lhs_y += [mm.astype(BF16), l2.astype(BF16)]
                rhs_y += [xm, pm]
                lhs_s.append(bw.astype(BF16))
                rhs_s.append(xm)
                dec = dec + jnp.where(emask[e], jnp.broadcast_to(etot[:, li:li + 1], (1, gw)), 0.0)
            ys.append(_dotf(jnp.concatenate(lhs_y, axis=-1), jnp.concatenate(rhs_y, axis=0)))
            sg = _dotf(jnp.concatenate(lhs_s, axis=-1), jnp.concatenate(rhs_s, axis=0))
            st_s[:, grp * gw:(grp + 1) * gw] = prev * dec + sg
        return jnp.concatenate(ys, axis=-1)

    st_s[...] = jnp.zeros_like(st_s)

    def fwd_body(c, carry):
        y_s[pl.ds(pl.multiple_of(c * q, q), q), :] = chunk_update(c, 0)
        return carry

    lax.fori_loop(0, nch, fwd_body, 0, unroll=2)

    st_s[...] = jnp.zeros_like(st_s)

    def finish(c, yb, zz):
        rows = pl.ds(pl.multiple_of(c * q, q), q)
        y = y_s[rows, :] + yb + u_s[rows, 0:SSM_D_INNER] * dsk_ref[...]
        y = y * _silu(zz.astype(F32))
        return _rms_rows(y, ng_ref[...]).astype(BF16)

    def bwd_ctx_body(s, carry):
        c = ncc - 1 - s
        yb = chunk_update(c, 1)
        if ctx_out:
            rows = pl.ds(pl.multiple_of(c * q, q), q)
            oc_ref[0, rows, :] = finish(c, yb, zc_ref[0, rows, :])
        return carry

    lax.fori_loop(0, ncc, bwd_ctx_body, 0)

    def bwd_lat_body(s, carry):
        c = nch - 1 - s
        yb = chunk_update(c, 1)
        rows = pl.ds(pl.multiple_of((c - ncc) * q, q), q)
        o_ref[0, rows, :] = finish(c, yb, z_ref[0, rows, :])
        return carry

    lax.fori_loop(0, nch - ncc, bwd_lat_body, 0, unroll=2)


def _ssd(p, pc, conv_w, conv_b, dt_bias, a_log, d_skip, norm_gain, ctx_out):
    bsz, n, _ = p.shape
    nc = pc.shape[1]
    di = SSM_D_INNER
    q = SSM_CHUNK
    padl = lambda v: jnp.pad(v.astype(F32).reshape(1, -1), ((0, 0), (0, LANE - 2 * SSM_HEADS)))
    dtb = padl(dt_bias)
    alog = padl(a_log)
    dsk = jnp.repeat(d_skip.astype(F32), SSM_HEAD_DIM).reshape(1, di)
    ng = norm_gain.astype(F32).reshape(1, di)
    tri = jnp.asarray(np.tril(np.ones((q, q))), BF16)
    z_blk = (3 * NA_WIDTH + 3 * DF_WIDTH) // di
    x_blk = (3 * NA_WIDTH + 3 * DF_WIDTH + di) // SSM_CONV_DIM
    spec = lambda nrow, w, j: pl.BlockSpec((1, nrow, w), lambda b: (b, 0, j))
    const = lambda shape: pl.BlockSpec(shape, lambda b: (0,) * len(shape))
    in_specs = [spec(n, di, z_blk), spec(n, SSM_CONV_DIM, x_blk), spec(n, LANE, DT_BLOCK)]
    args = [p, p, p]
    if ctx_out:
        in_specs.append(spec(nc, di, z_blk))
        args.append(pc)
    in_specs += [spec(nc, SSM_CONV_DIM, x_blk), spec(nc, LANE, DT_BLOCK),
                 const((SSM_CONV, SSM_CONV_DIM)), const((1, SSM_CONV_DIM)), const((1, LANE)), const((1, LANE)),
                 const((1, di)), const((1, di)), const((q, q)), const(((SSM_CONV - 1) * q, CONV_WIN))]
    side = [k for k in range(SSM_CONV) if k != SSM_CONV // 2]
    sel = np.zeros((len(side) * q, CONV_WIN), np.float32)
    for j, k in enumerate(side):
        sel[j * q + np.arange(q), CONV_PAD + np.arange(q) + k - SSM_CONV // 2] = 1.0
    args += [pc, pc, conv_w.astype(F32), conv_b.astype(F32).reshape(1, -1), dtb, alog, dsk, ng, tri,
             jnp.asarray(sel, BF16)]
    out_specs = [pl.BlockSpec((1, n, di), lambda b: (b, 0, 0))]
    out_shape = [jax.ShapeDtypeStruct((bsz, n, di), BF16)]
    if ctx_out:
        out_specs.append(pl.BlockSpec((1, nc, di), lambda b: (b, 0, 0)))
        out_shape.append(jax.ShapeDtypeStruct((bsz, nc, di), BF16))
    tot = n + nc
    xpad_rows = tot - q + CONV_PAD + CONV_WIN
    scratch = [pltpu.VMEM((xpad_rows, SSM_CONV_DIM), BF16), pltpu.VMEM((tot, SSM_CONV_DIM), F32),
               pltpu.VMEM((tot, LANE), F32), pltpu.VMEM((tot, LANE), F32), pltpu.VMEM((tot, di), F32),
               pltpu.VMEM((SSM_STATE, di), F32),
               pltpu.VMEM((tot // q, 2 * SSM_HEADS, q), F32), pltpu.VMEM((tot // q, 2 * SSM_HEADS, q), F32),
               pltpu.VMEM((tot // q, SSM_GROUPS, SSM_STATE, q), F32)]
    outs = pl.pallas_call(
        functools.partial(_ssd_kernel, ctx_out=ctx_out),
        grid=(bsz,),
        in_specs=in_specs,
        out_specs=out_specs,
        out_shape=out_shape,
        scratch_shapes=scratch,
        compiler_params=_cparams(("parallel",)),
        name="ssd",
    )(*args)
    return (outs[0], outs[1]) if ctx_out else (outs[0], None)


def _mlp_kernel(x_ref, xp_ref, xn_ref, a_ref, ap_ref, an_ref, b_ref, bp_ref, bn_ref, c_ref, cp_ref, cn_ref,
                g1_ref, sh_ref, sc_ref, g2_ref, gn_ref, wo_ref, wu_ref, cw_ref, cb_ref, wd_ref, o_ref, *, n_tiles):
    i = pl.program_id(1)
    tm = x_ref.shape[1]
    halo = xp_ref.shape[1]
    ext = tm + 2 * halo
    y_ext = jnp.concatenate([
        jnp.concatenate([ap_ref[0], bp_ref[0], cp_ref[0]], axis=-1),
        jnp.concatenate([a_ref[0], b_ref[0], c_ref[0]], axis=-1),
        jnp.concatenate([an_ref[0], bn_ref[0], cn_ref[0]], axis=-1)], axis=0)
    x_ext = jnp.concatenate([xp_ref[0], x_ref[0], xn_ref[0]], axis=0)
    x1_ext = x_ext + g1_ref[0] * _dotf(y_ext, wo_ref[0])
    h = _rms_rows(x1_ext, gn_ref[...]) * (1.0 + sc_ref[0]) + sh_ref[0]
    r = lax.broadcasted_iota(jnp.int32, (ext, 1), 0)
    keep = jnp.where(r < halo, jnp.where(i > 0, 1.0, 0.0),
                     jnp.where(r >= halo + tm, jnp.where(i < n_tiles - 1, 1.0, 0.0), 1.0))
    h_ext = (h * keep).astype(BF16)
    h_mid = h_ext[halo:halo + tm]
    x1 = x1_ext[halo:halo + tm]
    acc = jnp.zeros((tm, D_MODEL), F32)
    lo = 0
    for width in FFN_CHUNKS:
        cols = slice(lo, lo + width)
        gate = _dotf(h_ext, wu_ref[0, :, cols])
        val = _dotf(h_mid, wu_ref[0, :, D_FF + lo:D_FF + lo + width])
        conv = (cb_ref[:, cols]
                + cw_ref[0:1, cols] * pltpu.roll(gate, 1, 0)[halo:halo + tm]
                + cw_ref[1:2, cols] * gate[halo:halo + tm]
                + cw_ref[2:3, cols] * pltpu.roll(gate, ext - 1, 0)[halo:halo + tm])
        act = (_silu(conv) * val).astype(BF16)
        acc = acc + _dotf(act, wd_ref[0, cols, :])
        lo += width
    o_ref[0] = x1 + g2_ref[0] * acc


def _mlp(x, y_na, y_df, y_ssm, mod, gain, w_out, w_up, conv_w, conv_b, w_down, layer, fixed_row):
    bsz, n, _ = x.shape
    tm = min(FFN_TM, n)
    n_tiles = n // tm
    halo = MLP_HALO
    hb = tm // halo
    last = n // halo - 1
    const = lambda shape: pl.BlockSpec((1,) + shape, lambda b, i: (layer,) + (0,) * len(shape),
                                       pipeline_mode=pl.Buffered(1))

    def rows3(width):
        return [pl.BlockSpec((1, tm, width), lambda b, i: (b, i, 0)),
                pl.BlockSpec((1, halo, width), lambda b, i: (b, jnp.maximum(i * hb - 1, 0), 0)),
                pl.BlockSpec((1, halo, width), lambda b, i: (b, jnp.minimum((i + 1) * hb, last), 0))]

    return pl.pallas_call(
        functools.partial(_mlp_kernel, n_tiles=n_tiles),
        grid=(bsz, n_tiles),
        in_specs=rows3(D_MODEL) + rows3(NA_WIDTH) + rows3(DF_WIDTH) + rows3(SSM_D_INNER) + [
            _mod_spec(2, fixed_row),
            _mod_spec(3, fixed_row),
            _mod_spec(4, fixed_row),
            _mod_spec(5, fixed_row),
            pl.BlockSpec((1, D_MODEL), lambda b, i: (0, 0)),
            const((D_MIX, D_MODEL)),
            const((D_MODEL, 2 * D_FF)),
            pl.BlockSpec((FFN_CONV, D_FF), lambda b, i: (0, 0)),
            pl.BlockSpec((1, D_FF), lambda b, i: (0, 0)),
            const((D_FF, D_MODEL)),
        ],
        out_specs=pl.BlockSpec((1, tm, D_MODEL), lambda b, i: (b, i, 0)),
        out_shape=jax.ShapeDtypeStruct((bsz, n, D_MODEL), F32),
        compiler_params=_cparams(("parallel", "parallel")),
        name="mlp",
    )(x, x, x, y_na, y_na, y_na, y_df, y_df, y_df, y_ssm, y_ssm, y_ssm, mod, mod, mod, mod, gain,
      w_out, w_up, conv_w, conv_b, w_down)


def kernel(x, c, ctx, c_ctx, w_ada, b_ada, g_mix, g_ffn, w_in, na_q_gain, na_k_gain, na_rpb, df_q_gain, df_k_gain, df_lambda, df_subln, ssm_conv_w, ssm_conv_b, ssm_dt_bias, ssm_a_log, ssm_d, ssm_norm, w_out, ffn_w_up, ffn_conv_w, ffn_conv_b, ffn_w_down):
    bsz, n, _ = x.shape
    ctx_row = bsz
    mod_rows = -(-(bsz + 1) // SUBLANE) * SUBLANE
    cc = jnp.concatenate([c, c_ctx[None, :], jnp.zeros((mod_rows - bsz - 1, D_MODEL), c.dtype)], axis=0)
    mods = _ada(cc.astype(F32), w_ada.astype(F32), b_ada.astype(F32))
    mods = mods.reshape(DEPTH, mod_rows, 1, 6 * D_MODEL)
    tables = _rope_tables(n)
    w_in_b = jnp.pad(w_in, ((0, 0), (0, 0), (0, IN_COLS_PAD - IN_COLS))).astype(BF16)
    w_out_b = w_out.astype(BF16)
    w_up_b = ffn_w_up.astype(BF16)
    w_down_b = ffn_w_down.astype(BF16)
    x = x.astype(F32)
    ctx = ctx.astype(F32)
    for l in range(DEPTH):
        ctx_out = l < DEPTH - 1
        mod = mods[l]
        gm = g_mix[l].astype(F32).reshape(1, D_MODEL)
        gf = g_ffn[l].astype(F32).reshape(1, D_MODEL)
        p = _inproj(x, mod, gm, w_in_b, l, None)
        pc = _inproj(ctx, mod, gm, w_in_b, l, ctx_row)
        y_na, yc_na = _na(p, pc, na_q_gain[l], na_k_gain[l], na_rpb[l], ctx_out)
        y_df, yc_df = _df(p, pc, tables, df_q_gain[l], df_k_gain[l], df_lambda[l], df_subln[l], l, ctx_out)
        y_ssm, yc_ssm = _ssd(p, pc, ssm_conv_w[l], ssm_conv_b[l], ssm_dt_bias[l], ssm_a_log[l], ssm_d[l],
                             ssm_norm[l], ctx_out)
        cw = ffn_conv_w[l].astype(F32)
        cb = ffn_conv_b[l].astype(F32).reshape(1, D_FF)
        x = _mlp(x, y_na, y_df, y_ssm, mod, gf, w_out_b, w_up_b, cw, cb, w_down_b, l, None)
        if ctx_out:
            ctx = _mlp(ctx, yc_na, yc_df, yc_ssm, mod, gf, w_out_b, w_up_b, cw, cb, w_down_b, l, ctx_row)
    return x
```

```python
import functools
import math

import numpy as np
import jax
import jax.numpy as jnp
from jax import lax
from jax.experimental import pallas as pl
from jax.experimental.pallas import tpu as pltpu

F32 = jnp.float32
BF16 = jnp.bfloat16

D_MODEL = 1024
DEPTH = 2
GRID_W = 64
NORM_EPS = 1e-6
NA_HEADS = 4
NA_HEAD_DIM = 64
NA_ROWS = 8
NA_COLS = 16
NA_WIDTH = NA_HEADS * NA_HEAD_DIM
DF_HEADS = 4
DF_HEAD_DIM = 32
DF_WIDTH = DF_HEADS * 2 * DF_HEAD_DIM
ROPE_BASE = 10000.0
SSM_HEADS = 8
SSM_HEAD_DIM = 64
SSM_D_INNER = SSM_HEADS * SSM_HEAD_DIM
SSM_GROUPS = 2
SSM_STATE = 128
SSM_CONV = 5
SSM_CHUNK = 128
SSM_CONV_DIM = SSM_D_INNER + 2 * SSM_GROUPS * SSM_STATE
D_MIX = NA_WIDTH + DF_WIDTH + SSM_D_INNER
IN_COLS = 3 * NA_WIDTH + 3 * DF_WIDTH + SSM_D_INNER + SSM_CONV_DIM + 2 * SSM_HEADS
D_FF = 2816
FFN_CONV = 3

LANE = 128
SUBLANE = 8
IN_COLS_PAD = -(-IN_COLS // LANE) * LANE
DT_BLOCK = IN_COLS_PAD // LANE - 1
VMEM_LIMIT = 56 * 1024 * 1024

IN_TM = 1024
FFN_TM = 1024
FFN_CHUNKS = (1536, 1280)
MLP_HALO = 16
DF_TQ = 128
DF_UNROLL = 8
DF_KBLOCK = 768
DF_KCHUNK = 128
DF_VT_ROWS =2 * DF_HEAD_DIM + 16
CONV_STRIP = 2 * LANE
CONV_PAD = 16
CONV_WIN = 2 * SSM_CHUNK
NA_UNROLL = 16
LOG2E = math.log2(math.e)


def _cparams(sem):
    return pltpu.CompilerParams(dimension_semantics=sem, vmem_limit_bytes=VMEM_LIMIT)


def _dotf(a, b):
    return jnp.dot(a, b, preferred_element_type=F32)


def _dot_nt(a, b):
    return lax.dot_general(a, b, (((1,), (1,)), ((), ())), preferred_element_type=F32)


def _split2(x):
    hi = x.astype(BF16)
    lo = (x - hi.astype(F32)).astype(BF16)
    return hi, lo


def _split3(x):
    hi = x.astype(BF16)
    r = x - hi.astype(F32)
    mid = r.astype(BF16)
    lo = (r - mid.astype(F32)).astype(BF16)
    return hi, mid, lo


def _segmean(xsq, bd):
    hi, lo = _split2(xsq)
    return _dotf(hi, bd) + _dotf(lo, bd)


def _sigmoid(x):
    return 1.0 / (1.0 + jnp.exp(-x))


def _silu(x):
    return x * _sigmoid(x)


def _rms_rows(xf, gain):
    ms = jnp.mean(xf * xf, axis=-1, keepdims=True)
    return xf * lax.rsqrt(ms + NORM_EPS) * gain


def _ada_kernel(c_ref, w_ref, b_ref, o_ref):
    s = _silu(c_ref[...])
    shi, slo = _split2(s)
    whi, wlo = _split2(w_ref[0])
    o_ref[0] = _dotf(shi, whi) + _dotf(shi, wlo) + _dotf(slo, whi) + b_ref[0]


def _ada(cc, w_ada, b_ada):
    rows = cc.shape[0]
    nblk = 6
    return pl.pallas_call(
        _ada_kernel,
        grid=(DEPTH, nblk),
        in_specs=[
            pl.BlockSpec((rows, D_MODEL), lambda l, j: (0, 0)),
            pl.BlockSpec((1, D_MODEL, D_MODEL), lambda l, j: (l, 0, j)),
            pl.BlockSpec((1, 1, D_MODEL), lambda l, j: (l, 0, j)),
        ],
        out_specs=pl.BlockSpec((1, rows, D_MODEL), lambda l, j: (l, 0, j)),
        out_shape=jax.ShapeDtypeStruct((DEPTH, rows, 6 * D_MODEL), F32),
        compiler_params=_cparams(("parallel", "parallel")),
        name="adaln",
    )(cc, w_ada, b_ada.reshape(DEPTH, 1, 6 * D_MODEL))


def _mod_spec(chunk, fixed_row):
    if fixed_row is None:
        return pl.BlockSpec((1, 1, D_MODEL), lambda b, i: (b, 0, chunk))
    return pl.BlockSpec((1, 1, D_MODEL), lambda b, i: (fixed_row, 0, chunk))


def _inproj_kernel(x_ref, sh_ref, sc_ref, g_ref, w_ref, o_ref):
    h = _rms_rows(x_ref[0], g_ref[...]) * (1.0 + sc_ref[0]) + sh_ref[0]
    o_ref[0] = _dotf(h.astype(BF16), w_ref[0]).astype(BF16)


def _inproj(x, mod, gain, w, layer, fixed_row):
    bsz, n, _ = x.shape
    tm = min(IN_TM, n)
    return pl.pallas_call(
        _inproj_kernel,
        grid=(bsz, n // tm),
        in_specs=[
            pl.BlockSpec((1, tm, D_MODEL), lambda b, i: (b, i, 0)),
            _mod_spec(0, fixed_row),
            _mod_spec(1, fixed_row),
            pl.BlockSpec((1, D_MODEL), lambda b, i: (0, 0)),
            pl.BlockSpec((1, D_MODEL, IN_COLS_PAD), lambda b, i: (layer, 0, 0)),
        ],
        out_specs=pl.BlockSpec((1, tm, IN_COLS_PAD), lambda b, i: (b, i, 0)),
        out_shape=jax.ShapeDtypeStruct((bsz, n, IN_COLS_PAD), BF16),
        compiler_params=_cparams(("parallel", "parallel")),
        name="inproj",
    )(x, mod, mod, gain, w)


def _na_kernel(*refs, rows, ctx_out):
    if ctx_out:
        (q_ref, k_ref, v_ref, qc_ref, kc_ref, vc_ref, qg_ref, kg_ref, bias_ref, bd_ref,
         o_ref, oc_ref, qn_s, kn_s, kcn_s, qcn_s) = refs
    else:
        (q_ref, k_ref, v_ref, kc_ref, vc_ref, qg_ref, kg_ref, bias_ref, bd_ref,
         o_ref, qn_s, kn_s, kcn_s) = refs
    bd = bd_ref[...]
    scale = NA_HEAD_DIM ** -0.5 * LOG2E

    def norm(x, g):
        xf = x.astype(F32)
        return xf * lax.rsqrt(_segmean(xf * xf, bd) + NORM_EPS) * g

    qn_s[...] = (norm(q_ref[0], qg_ref[...]) * scale).astype(BF16)
    kn_s[...] = norm(k_ref[0], kg_ref[...]).astype(BF16)
    kcn_s[...] = norm(kc_ref[0], kg_ref[...]).astype(BF16)

    head = lax.broadcasted_iota(jnp.int32, (1, NA_WIDTH), 1) // NA_HEAD_DIM
    hmask = [head == h for h in range(NA_HEADS)]
    wr = min(NA_ROWS, rows)
    nloc = wr * GRID_W

    def stack_heads(q):
        return jnp.concatenate([jnp.where(hmask[h], q, jnp.zeros_like(q)) for h in range(NA_HEADS)], axis=0)

    def unstack_heads(o):
        out = jnp.where(hmask[0], o[0:GRID_W], 0.0)
        for h in range(1, NA_HEADS):
            out = out + jnp.where(hmask[h], o[h * GRID_W:(h + 1) * GRID_W], 0.0)
        return out

    def row_body(r, carry):
        kr0 = jnp.clip(r - wr // 2, 0, rows - wr)
        d0 = kr0 - r + NA_ROWS - 1
        q4 = stack_heads(qn_s[pl.ds(pl.multiple_of(r * GRID_W, GRID_W), GRID_W), :])
        kstart = pl.multiple_of(kr0 * GRID_W, GRID_W)
        s_loc = _dot_nt(q4, kn_s[pl.ds(kstart, nloc), :]) + bias_ref[d0]
        s_ctx = _dot_nt(q4, kcn_s[...])
        m = jnp.maximum(jnp.max(s_loc, axis=-1, keepdims=True), jnp.max(s_ctx, axis=-1, keepdims=True))
        p_loc = jnp.exp2(s_loc - m)
        p_ctx = jnp.exp2(s_ctx - m)
        l = jnp.sum(p_loc, axis=-1, keepdims=True) + jnp.sum(p_ctx, axis=-1, keepdims=True)
        o = _dotf(p_loc.astype(BF16), v_ref[0, pl.ds(kstart, nloc), :]) + _dotf(p_ctx.astype(BF16), vc_ref[0])
        o = o * (1.0 / l)
        o_ref[0, pl.ds(pl.multiple_of(r * GRID_W, GRID_W), GRID_W), :] = unstack_heads(o).astype(BF16)
        return carry

    lax.fori_loop(0, rows, row_body, 0, unroll=NA_UNROLL)

    if ctx_out:
        qcn_s[...] = (norm(qc_ref[0], qg_ref[...]) * scale).astype(BF16)
        nblk = qcn_s.shape[0] // GRID_W

        def ctx_body(r, carry):
            rs = pl.multiple_of(r * GRID_W, GRID_W)
            q4 = stack_heads(qcn_s[pl.ds(rs, GRID_W), :])
            s = _dot_nt(q4, kcn_s[...])
            m = jnp.max(s, axis=-1, keepdims=True)
            p = jnp.exp2(s - m)
            l = jnp.sum(p, axis=-1, keepdims=True)
            o = _dotf(p.astype(BF16), vc_ref[0]) * (1.0 / l)
            oc_ref[0, pl.ds(rs, GRID_W), :] = unstack_heads(o).astype(BF16)
            return carry

        lax.fori_loop(0, nblk, ctx_body, 0)


def _na_bias_table(rpb, rows):
    wr = min(NA_ROWS, rows)
    qc = np.arange(GRID_W)
    kc = np.arange(GRID_W)
    dc = np.clip(kc[None, :] - qc[:, None], 1 - NA_COLS, NA_COLS - 1) + NA_COLS - 1
    ws = np.clip(qc - NA_COLS // 2, 0, GRID_W - NA_COLS)
    ok = (kc[None, :] >= ws[:, None]) & (kc[None, :] < ws[:, None] + NA_COLS)
    onehot = jnp.asarray(dc[None, :, :] == np.arange(2 * NA_COLS - 1)[:, None, None], F32)
    t = jnp.einsum('hrc,cqk->hrqk', rpb.astype(F32), onehot, precision=lax.Precision.HIGHEST)
    t = jnp.where(ok[None, None], t, -jnp.inf)
    t = jnp.stack([t[:, d0:d0 + wr] for d0 in range(NA_ROWS)], axis=0)
    return t.transpose(0, 1, 3, 2, 4).reshape(NA_ROWS, NA_HEADS * GRID_W, wr * GRID_W) * LOG2E


def _na(p, pc, q_gain, k_gain, rpb, ctx_out):
    bsz, n, _ = p.shape
    nc = pc.shape[1]
    rows = n // GRID_W
    w = NA_WIDTH
    bias = _na_bias_table(rpb, rows)
    bd = jnp.asarray(np.kron(np.eye(NA_HEADS), np.full((NA_HEAD_DIM, NA_HEAD_DIM), 1.0 / NA_HEAD_DIM)), BF16)
    qg = jnp.tile(q_gain.astype(F32), NA_HEADS).reshape(1, w)
    kg = jnp.tile(k_gain.astype(F32), NA_HEADS).reshape(1, w)
    col = lambda nrow, j: pl.BlockSpec((1, nrow, w), lambda b: (b, 0, j))
    const = lambda shape: pl.BlockSpec(shape, lambda b: (0,) * len(shape))
    in_specs = [col(n, 0), col(n, 1), col(n, 2)]
    args = [p, p, p]
    if ctx_out:
        in_specs.append(col(nc, 0))
        args.append(pc)
    in_specs += [col(nc, 1), col(nc, 2), const((1, w)), const((1, w)), const(bias.shape), const((w, w))]
    args += [pc, pc, qg, kg, bias, bd]
    out_specs = [pl.BlockSpec((1, n, w), lambda b: (b, 0, 0))]
    out_shape = [jax.ShapeDtypeStruct((bsz, n, w), BF16)]
    scratch = [pltpu.VMEM((n, w), BF16), pltpu.VMEM((n, w), BF16), pltpu.VMEM((nc, w), BF16)]
    if ctx_out:
        out_specs.append(pl.BlockSpec((1, nc, w), lambda b: (b, 0, 0)))
        out_shape.append(jax.ShapeDtypeStruct((bsz, nc, w), BF16))
        scratch.append(pltpu.VMEM((nc, w), BF16))
    outs = pl.pallas_call(
        functools.partial(_na_kernel, rows=rows, ctx_out=ctx_out),
        grid=(bsz,),
        in_specs=in_specs,
        out_specs=out_specs,
        out_shape=out_shape,
        scratch_shapes=scratch,
        compiler_params=_cparams(("parallel",)),
        name="na_attn",
    )(*args)
    return (outs[0], outs[1]) if ctx_out else (outs[0], None)


def _rope_tables(n):
    per_axis = DF_HEAD_DIM // 2
    inv_freq = ROPE_BASE ** (-jnp.arange(0, per_axis, 2, dtype=F32) / per_axis)
    t = jnp.arange(n, dtype=jnp.int32)
    pos = jnp.stack([t // GRID_W, t % GRID_W], axis=-1).astype(F32)
    ang = pos[:, :, None] * inv_freq
    ang = jnp.concatenate([ang, ang], axis=-1).reshape(n, DF_HEAD_DIM)
    reps = DF_WIDTH // DF_HEAD_DIM
    cos = jnp.tile(jnp.cos(ang), (1, reps))
    sin = jnp.tile(jnp.sin(ang), (1, reps))
    first = (np.arange(DF_WIDTH) % (DF_HEAD_DIM // 2)) < DF_HEAD_DIM // 4
    sin_a = jnp.where(first[None, :], -sin, 0.0)
    sin_b = jnp.where(first[None, :], 0.0, sin)
    return cos, sin_a, sin_b


def _df_kernel(*refs, ctx_out, lam_init, tq):
    if ctx_out:
        (q_ref, k_ref, v_ref, qc_ref, kc_ref, vc_ref, cos_ref, sa_ref, sb_ref, qg_ref, kg_ref, sub_ref,
         lam_ref, bd32_ref, bd64_ref, o_ref, oc_ref, qn_s, kall_s, vt_s, qcn_s) = refs
    else:
        (q_ref, k_ref, v_ref, kc_ref, vc_ref, cos_ref, sa_ref, sb_ref, qg_ref, kg_ref, sub_ref,
         lam_ref, bd32_ref, bd64_ref, o_ref, qn_s, kall_s, vt_s) = refs
    n = q_ref.shape[1]
    nc = kc_ref.shape[1]
    bd32 = bd32_ref[...]
    bd64 = bd64_ref[...]
    scale = DF_HEAD_DIM ** -0.5 * LOG2E
    lam = lam_ref[0, 0]
    shift = DF_HEAD_DIM // 4

    def norm(x, g):
        xf = x.astype(F32)
        return xf * lax.rsqrt(_segmean(xf * xf, bd32) + NORM_EPS) * g

    def rope(xn):
        return (xn * cos_ref[...] + pltpu.roll(xn, DF_WIDTH - shift, 1) * sa_ref[...]
                + pltpu.roll(xn, shift, 1) * sb_ref[...])

    qn_s[...] = (rope(norm(q_ref[0], qg_ref[...])) * scale).T.astype(BF16)
    kall_s[0:n, :] = rope(norm(k_ref[0], kg_ref[...])).astype(BF16)
    kall_s[n:n + nc, :] = norm(kc_ref[0], kg_ref[...]).astype(BF16)
    hd = 2 * DF_HEAD_DIM
    tail_row = lax.broadcasted_iota(jnp.int32, (DF_VT_ROWS - hd, n + nc), 0)
    tail = jnp.where(tail_row == 0, 1.0, 0.0).astype(BF16)
    v_t = v_ref[0].astype(F32).T
    vc_t = vc_ref[0].astype(F32).T
    for h in range(DF_HEADS):
        vt_s[h, 0:hd, 0:n] = v_t[h * hd:(h + 1) * hd].astype(BF16)
        vt_s[h, 0:hd, n:n + nc] = vc_t[h * hd:(h + 1) * hd].astype(BF16)
        vt_s[h, hd:DF_VT_ROWS, :] = tail

    comp = lax.broadcasted_iota(jnp.int32, (DF_WIDTH, 1), 0) // DF_HEAD_DIM
    sub_gain = sub_ref[...] * (1.0 - lam_init)

    def attend(q, k_lo, k_hi):
        t = q.shape[1]
        zero = jnp.zeros_like(q)
        nk = k_hi - k_lo
        blk = min(DF_KBLOCK, nk)

        def scores(h):
            q2 = jnp.concatenate([jnp.where(comp == 2 * h, q, zero), jnp.where(comp == 2 * h + 1, q, zero)], axis=1)
            return [_dotf(kall_s[k_lo + b0:k_lo + b0 + blk, :], q2) for b0 in range(0, nk, blk)]

        parts = []
        s_next = scores(0)
        for h in range(DF_HEADS):
            s_blocks = s_next
            if h + 1 < DF_HEADS:
                s_next = scores(h + 1)
            m = acc = None
            for bi, s_blk in enumerate(s_blocks):
                for c0 in range(0, blk, DF_KCHUNK):
                    s_c = s_blk[c0:c0 + DF_KCHUNK]
                    k0 = k_lo + bi * blk + c0
                    m_c = jnp.max(s_c, axis=0, keepdims=True)
                    m_new = m_c if m is None else jnp.maximum(m, m_c)
                    p_c = jnp.exp2((s_c - m_new).astype(BF16))
                    o_c = _dotf(vt_s[h, :, k0:k0 + DF_KCHUNK], p_c)
                    acc = o_c if acc is None else acc * jnp.exp2(m - m_new) + o_c
                    m = m_new
            o = acc[0:hd] * (1.0 / acc[hd:hd + 1])
            parts.append(o[:, 0:t] - lam * o[:, t:2 * t])
        y = jnp.concatenate(parts, axis=0).T
        return y * lax.rsqrt(_segmean(y * y, bd64) + NORM_EPS) * sub_gain

    def q_body(i, carry):
        rs = pl.multiple_of(i * tq, tq)
        o_ref[0, pl.ds(rs, tq), :] = attend(qn_s[:, pl.ds(rs, tq)], 0, n + nc).astype(BF16)
        return carry

    lax.fori_loop(0, n // tq, q_body, 0, unroll=DF_UNROLL)

    if ctx_out:
        qcn_s[...] = (norm(qc_ref[0], qg_ref[...]) * scale).T.astype(BF16)
        tc = min(tq, nc)

        def c_body(i, carry):
            rs = pl.multiple_of(i * tc, tc)
            oc_ref[0, pl.ds(rs, tc), :] = attend(qcn_s[:, pl.ds(rs, tc)], n, n + nc).astype(BF16)
            return carry

        lax.fori_loop(0, nc // tc, c_body, 0)


def _df(p, pc, tables, q_gain, k_gain, lam_vecs, subln, layer, ctx_out):
    bsz, n, _ = p.shape
    nc = pc.shape[1]
    w = DF_WIDTH
    cos, sin_a, sin_b = tables
    lam_init = 0.8 - 0.6 * math.exp(-0.3 * layer)
    lv = lam_vecs.astype(F32)
    lam = (jnp.exp(jnp.sum(lv[0] * lv[1])) - jnp.exp(jnp.sum(lv[2] * lv[3])) + lam_init).reshape(1, 1)
    bd32 = jnp.asarray(np.kron(np.eye(w // DF_HEAD_DIM), np.full((DF_HEAD_DIM,) * 2, 1.0 / DF_HEAD_DIM)), BF16)
    bd64 = jnp.asarray(np.kron(np.eye(DF_HEADS), np.full((2 * DF_HEAD_DIM,) * 2, 0.5 / DF_HEAD_DIM)), BF16)
    qg = jnp.tile(q_gain.astype(F32), w // DF_HEAD_DIM).reshape(1, w)
    kg = jnp.tile(k_gain.astype(F32), w // DF_HEAD_DIM).reshape(1, w)
    sub = jnp.tile(subln.astype(F32), DF_HEADS).reshape(1, w)
    col = lambda nrow, j: pl.BlockSpec((1, nrow, w), lambda b: (b, 0, j))
    const = lambda shape: pl.BlockSpec(shape, lambda b: (0,) * len(shape))
    in_specs = [col(n, 3), col(n, 4), col(n, 5)]
    args = [p, p, p]
    if ctx_out:
        in_specs.append(col(nc, 3))
        args.append(pc)
    in_specs += [col(nc, 4), col(nc, 5), const((n, w)), const((n, w)), const((n, w)),
                 const((1, w)), const((1, w)), const((1, w)),
                 pl.BlockSpec(memory_space=pltpu.SMEM), const((w, w)), const((w, w))]
    args += [pc, pc, cos, sin_a, sin_b, qg, kg, sub, lam, bd32, bd64]
    out_specs = [pl.BlockSpec((1, n, w), lambda b: (b, 0, 0))]
    out_shape = [jax.ShapeDtypeStruct((bsz, n, w), BF16)]
    scratch = [pltpu.VMEM((w, n), BF16), pltpu.VMEM((n + nc, w), BF16),
               pltpu.VMEM((DF_HEADS, DF_VT_ROWS, n + nc), BF16)]
    if ctx_out:
        out_specs.append(pl.BlockSpec((1, nc, w), lambda b: (b, 0, 0)))
        out_shape.append(jax.ShapeDtypeStruct((bsz, nc, w), BF16))
        scratch.append(pltpu.VMEM((w, nc), BF16))
    outs = pl.pallas_call(
        functools.partial(_df_kernel, ctx_out=ctx_out, lam_init=lam_init, tq=DF_TQ),
        grid=(bsz,),
        in_specs=in_specs,
        out_specs=out_specs,
        out_shape=out_shape,
        scratch_shapes=scratch,
        compiler_params=_cparams(("parallel",)),
        name="diff_attn",
    )(*args)
    return (outs[0], outs[1]) if ctx_out else (outs[0], None)


def _ssd_kernel(*refs, ctx_out):
    if ctx_out:
        (z_ref, xbc_ref, dt_ref, zc_ref, xbcc_ref, dtc_ref, cw_ref, cb_ref, dtb_ref, alog_ref, dsk_ref, ng_ref,
         tri_ref, shift_ref, o_ref, oc_ref, xpad_s, u_s, g_s, dtv_s, y_s, st_s, gt_s, wt_s, bmt_s) = refs
    else:
        (z_ref, xbc_ref, dt_ref, xbcc_ref, dtc_ref, cw_ref, cb_ref, dtb_ref, alog_ref, dsk_ref, ng_ref,
         tri_ref, shift_ref, o_ref, xpad_s, u_s, g_s, dtv_s, y_s, st_s, gt_s, wt_s, bmt_s) = refs
        zc_ref = oc_ref = None
    n = xbc_ref.shape[1]
    nc = xbcc_ref.shape[1]
    q = SSM_CHUNK
    ncc = nc // q
    nch = (n + nc) // q
    pad = CONV_PAD
    cdim = SSM_CONV_DIM
    lat0 = 2 * pad + nc

    xpad_s[0:pad, :] = jnp.zeros((pad, cdim), BF16)
    xpad_s[pad:pad + nc, :] = xbcc_ref[0]
    xpad_s[pad + nc:lat0, :] = jnp.zeros((pad, cdim), BF16)
    xpad_s[lat0:lat0 + n, :] = xbc_ref[0]
    xpad_s[lat0 + n:, :] = jnp.zeros((xpad_s.shape[0] - lat0 - n, cdim), BF16)
    dtv_s[0:nc, :] = dtc_ref[0].astype(F32)
    dtv_s[nc:nc + n, :] = dt_ref[0].astype(F32)

    lane = lax.broadcasted_iota(jnp.int32, (1, LANE), 1)
    a_row = -jnp.exp(alog_ref[...]) * LOG2E
    tri = tri_ref[...]
    shifts = shift_ref[...]
    half = SSM_CONV // 2
    side_taps = [k for k in range(SSM_CONV) if k != half]

    def prep_body(c, carry):
        rb = pl.multiple_of(c * q + jnp.where(c >= ncc, pad, 0), pad)
        rows = pl.ds(pl.multiple_of(c * q, q), q)
        for s0 in range(0, cdim, CONV_STRIP):
            strip = slice(s0, s0 + CONV_STRIP)
            taps = _dotf(shifts, xpad_s[pl.ds(rb, CONV_WIN), strip])
            centre = xpad_s[pl.ds(pl.multiple_of(rb + pad, pad), q), strip].astype(F32)
            acc = cb_ref[:, strip] + cw_ref[half:half + 1, strip] * centre
            for j, k in enumerate(side_taps):
                acc = acc + cw_ref[k:k + 1, strip] * taps[j * q:(j + 1) * q]
            u_s[rows, strip] = _silu(acc)
        raw = dtv_s[rows, :] + dtb_ref[...]
        dt = jnp.maximum(raw, 0.0) + jnp.log(1.0 + jnp.exp(-jnp.abs(raw)))
        dt = jnp.where(lane < 2 * SSM_HEADS, dt, 0.0)
        la = dt * a_row
        h3 = _split3(la)
        cs = _dotf(tri, h3[0]) + _dotf(tri, h3[1]) + _dotf(tri, h3[2])
        tot = cs[q - 1:q, :]
        g = jnp.where(lane < SSM_HEADS, cs, tot - cs + la)
        g_s[rows, :] = g
        live = lane < 2 * SSM_HEADS
        glog = jnp.where(live, g - jnp.log2(dt), 0.0)
        gt_s[c] = glog.T[0:2 * SSM_HEADS]
        wt_s[c] = jnp.where(live, jnp.exp2(tot - g) * dt, 0.0).T[0:2 * SSM_HEADS]
        for grp in range(SSM_GROUPS):
            bmt_s[c, grp] = u_s[rows, SSM_D_INNER + grp * SSM_STATE:SSM_D_INNER + (grp + 1) * SSM_STATE].T
        return carry

    lax.fori_loop(0, nch, prep_body, 0, unroll=2)

    ii = lax.broadcasted_iota(jnp.int32, (q, q), 0)
    jj = lax.broadcasted_iota(jnp.int32, (q, q), 1)
    hpg = SSM_HEADS // SSM_GROUPS
    gw = hpg * SSM_HEAD_DIM
    hl = lax.broadcasted_iota(jnp.int32, (1, gw), 1) // SSM_HEAD_DIM
    emask = [hl == e for e in range(hpg)]

    def chunk_update(c, d):
        rows = pl.ds(pl.multiple_of(c * q, q), q)
        g = g_s[rows, :]
        tot = jnp.where(lane < SSM_HEADS, g[q - 1:q, :], g[0:1, :])
        etot = jnp.exp2(tot)
        gl_t = gt_s[c]
        w_t = wt_s[c]
        keep = (ii >= jj) if d == 0 else (jj >= ii)
        ys = []
        for grp in range(SSM_GROUPS):
            bm = u_s[rows, SSM_D_INNER + grp * SSM_STATE:SSM_D_INNER + (grp + 1) * SSM_STATE]
            cm = u_s[rows, SSM_D_INNER + (SSM_GROUPS + grp) * SSM_STATE:
                     SSM_D_INNER + (SSM_GROUPS + grp + 1) * SSM_STATE]
            xs = u_s[rows, grp * gw:(grp + 1) * gw].astype(BF16)
            prev = st_s[:, grp * gw:(grp + 1) * gw]
            prev_b = prev.astype(BF16)
            cb = _dot_nt(cm.astype(BF16), bm.astype(BF16))
            bm_t = bmt_s[c, grp]
            zero = jnp.zeros_like(xs)
            lhs_y, rhs_y, lhs_s, rhs_s = [], [], [], []
            dec = jnp.zeros((1, gw), F32)
            for e in range(hpg):
                li = d * SSM_HEADS + grp * hpg + e
                gcol = jnp.broadcast_to(g[:, li:li + 1], (q, q))
                grow = jnp.broadcast_to(gl_t[li:li + 1, :], (q, q))
                mm = cb * jnp.exp2(jnp.where(keep, gcol - grow, -jnp.inf))
                l2 = jnp.exp2(gcol) * cm
                bw = bm_t * jnp.broadcast_to(w_t[li:li + 1, :], (q, q))
                xm = jnp.where(emask[e], xs, zero)
                pm = jnp.where(emask[e], prev_b, zero)
                lhs_y += [mm.astype(BF16), l2.astype(BF16)]
                rhs_y += [xm, pm]
                lhs_s.append(bw.astype(BF16))
                rhs_s.append(xm)
                dec = dec + jnp.where(emask[e], jnp.broadcast_to(etot[:, li:li + 1], (1, gw)), 0.0)
            ys.append(_dotf(jnp.concatenate(lhs_y, axis=-1), jnp.concatenate(rhs_y, axis=0)))
            sg = _dotf(jnp.concatenate(lhs_s, axis=-1), jnp.concatenate(rhs_s, axis=0))
            st_s[:, grp * gw:(grp + 1) * gw] = prev * dec + sg
        return jnp.concatenate(ys, axis=-1)

    st_s[...] = jnp.zeros_like(st_s)

    def fwd_body(c, carry):
        y_s[pl.ds(pl.multiple_of(c * q, q), q), :] = chunk_update(c, 0)
        return carry

    lax.fori_loop(0, nch, fwd_body, 0, unroll=2)

    st_s[...] = jnp.zeros_like(st_s)

    def finish(c, yb, zz):
        rows = pl.ds(pl.multiple_of(c * q, q), q)
        y = y_s[rows, :] + yb + u_s[rows, 0:SSM_D_INNER] * dsk_ref[...]
        y = y * _silu(zz.astype(F32))
        return _rms_rows(y, ng_ref[...]).astype(BF16)

    def bwd_ctx_body(s, carry):
        c = ncc - 1 - s
        yb = chunk_update(c, 1)
        if ctx_out:
            rows = pl.ds(pl.multiple_of(c * q, q), q)
            oc_ref[0, rows, :] = finish(c, yb, zc_ref[0, rows, :])
        return carry

    lax.fori_loop(0, ncc, bwd_ctx_body, 0)

    def bwd_lat_body(s, carry):
        c = nch - 1 - s
        yb = chunk_update(c, 1)
        rows = pl.ds(pl.multiple_of((c - ncc) * q, q), q)
        o_ref[0, rows, :] = finish(c, yb, z_ref[0, rows, :])
        return carry

    lax.fori_loop(0, nch - ncc, bwd_lat_body, 0, unroll=2)


def _ssd(p, pc, conv_w, conv_b, dt_bias, a_log, d_skip, norm_gain, ctx_out):
    bsz, n, _ = p.shape
    nc = pc.shape[1]
    di = SSM_D_INNER
    q = SSM_CHUNK
    padl = lambda v: jnp.pad(v.astype(F32).reshape(1, -1), ((0, 0), (0, LANE - 2 * SSM_HEADS)))
    dtb = padl(dt_bias)
    alog = padl(a_log)
    dsk = jnp.repeat(d_skip.astype(F32), SSM_HEAD_DIM).reshape(1, di)
    ng = norm_gain.astype(F32).reshape(1, di)
    tri = jnp.asarray(np.tril(np.ones((q, q))), BF16)
    z_blk = (3 * NA_WIDTH + 3 * DF_WIDTH) // di
    x_blk = (3 * NA_WIDTH + 3 * DF_WIDTH + di) // SSM_CONV_DIM
    spec = lambda nrow, w, j: pl.BlockSpec((1, nrow, w), lambda b: (b, 0, j))
    const = lambda shape: pl.BlockSpec(shape, lambda b: (0,) * len(shape))
    in_specs = [spec(n, di, z_blk), spec(n, SSM_CONV_DIM, x_blk), spec(n, LANE, DT_BLOCK)]
    args = [p, p, p]
    if ctx_out:
        in_specs.append(spec(nc, di, z_blk))
        args.append(pc)
    in_specs += [spec(nc, SSM_CONV_DIM, x_blk), spec(nc, LANE, DT_BLOCK),
                 const((SSM_CONV, SSM_CONV_DIM)), const((1, SSM_CONV_DIM)), const((1, LANE)), const((1, LANE)),
                 const((1, di)), const((1, di)), const((q, q)), const(((SSM_CONV - 1) * q, CONV_WIN))]
    side = [k for k in range(SSM_CONV) if k != SSM_CONV // 2]
    sel = np.zeros((len(side) * q, CONV_WIN), np.float32)
    for j, k in enumerate(side):
        sel[j * q + np.arange(q), CONV_PAD + np.arange(q) + k - SSM_CONV // 2] = 1.0
    args += [pc, pc, conv_w.astype(F32), conv_b.astype(F32).reshape(1, -1), dtb, alog, dsk, ng, tri,
             jnp.asarray(sel, BF16)]
    out_specs = [pl.BlockSpec((1, n, di), lambda b: (b, 0, 0))]
    out_shape = [jax.ShapeDtypeStruct((bsz, n, di), BF16)]
    if ctx_out:
        out_specs.append(pl.BlockSpec((1, nc, di), lambda b: (b, 0, 0)))
        out_shape.append(jax.ShapeDtypeStruct((bsz, nc, di), BF16))
    tot = n + nc
    xpad_rows = tot - q + CONV_PAD + CONV_WIN
    scratch = [pltpu.VMEM((xpad_rows, SSM_CONV_DIM), BF16), pltpu.VMEM((tot, SSM_CONV_DIM), F32),
               pltpu.VMEM((tot, LANE), F32), pltpu.VMEM((tot, LANE), F32), pltpu.VMEM((tot, di), F32),
               pltpu.VMEM((SSM_STATE, di), F32),
               pltpu.VMEM((tot // q, 2 * SSM_HEADS, q), F32), pltpu.VMEM((tot // q, 2 * SSM_HEADS, q), F32),
               pltpu.VMEM((tot // q, SSM_GROUPS, SSM_STATE, q), F32)]
    outs = pl.pallas_call(
        functools.partial(_ssd_kernel, ctx_out=ctx_out),
        grid=(bsz,),
        in_specs=in_specs,
        out_specs=out_specs,
        out_shape=out_shape,
        scratch_shapes=scratch,
        compiler_params=_cparams(("parallel",)),
        name="ssd",
    )(*args)
    return (outs[0], outs[1]) if ctx_out else (outs[0], None)


def _mlp_kernel(x_ref, xp_ref, xn_ref, a_ref, ap_ref, an_ref, b_ref, bp_ref, bn_ref, c_ref, cp_ref, cn_ref,
                g1_ref, sh_ref, sc_ref, g2_ref, gn_ref, wo_ref, wu_ref, cw_ref, cb_ref, wd_ref, o_ref, *, n_tiles):
    i = pl.program_id(1)
    tm = x_ref.shape[1]
    halo = xp_ref.shape[1]
    ext = tm + 2 * halo
    y_ext = jnp.concatenate([
        jnp.concatenate([ap_ref[0], bp_ref[0], cp_ref[0]], axis=-1),
        jnp.concatenate([a_ref[0], b_ref[0], c_ref[0]], axis=-1),
        jnp.concatenate([an_ref[0], bn_ref[0], cn_ref[0]], axis=-1)], axis=0)
    x_ext = jnp.concatenate([xp_ref[0], x_ref[0], xn_ref[0]], axis=0)
    x1_ext = x_ext + g1_ref[0] * _dotf(y_ext, wo_ref[0])
    h = _rms_rows(x1_ext, gn_ref[...]) * (1.0 + sc_ref[0]) + sh_ref[0]
    r = lax.broadcasted_iota(jnp.int32, (ext, 1), 0)
    keep = jnp.where(r < halo, jnp.where(i > 0, 1.0, 0.0),
                     jnp.where(r >= halo + tm, jnp.where(i < n_tiles - 1, 1.0, 0.0), 1.0))
    h_ext = (h * keep).astype(BF16)
    h_mid = h_ext[halo:halo + tm]
    x1 = x1_ext[halo:halo + tm]
    acc = jnp.zeros((tm, D_MODEL), F32)
    lo = 0
    for width in FFN_CHUNKS:
        cols = slice(lo, lo + width)
        gate = _dotf(h_ext, wu_ref[0, :, cols])
        val = _dotf(h_mid, wu_ref[0, :, D_FF + lo:D_FF + lo + width])
        conv = (cb_ref[:, cols]
                + cw_ref[0:1, cols] * pltpu.roll(gate, 1, 0)[halo:halo + tm]
                + cw_ref[1:2, cols] * gate[halo:halo + tm]
                + cw_ref[2:3, cols] * pltpu.roll(gate, ext - 1, 0)[halo:halo + tm])
        act = (_silu(conv) * val).astype(BF16)
        acc = acc + _dotf(act, wd_ref[0, cols, :])
        lo += width
    o_ref[0] = x1 + g2_ref[0] * acc


def _mlp(x, y_na, y_df, y_ssm, mod, gain, w_out, w_up, conv_w, conv_b, w_down, layer, fixed_row):
    bsz, n, _ = x.shape
    tm = min(FFN_TM, n)
    n_tiles = n // tm
    halo = MLP_HALO
    hb = tm // halo
    last = n // halo - 1
    const = lambda shape: pl.BlockSpec((1,) + shape, lambda b, i: (layer,) + (0,) * len(shape),
                                       pipeline_mode=pl.Buffered(1))

    def rows3(width):
        return [pl.BlockSpec((1, tm, width), lambda b, i: (b, i, 0)),
                pl.BlockSpec((1, halo, width), lambda b, i: (b, jnp.maximum(i * hb - 1, 0), 0)),
                pl.BlockSpec((1, halo, width), lambda b, i: (b, jnp.minimum((i + 1) * hb, last), 0))]

    return pl.pallas_call(
        functools.partial(_mlp_kernel, n_tiles=n_tiles),
        grid=(bsz, n_tiles),
        in_specs=rows3(D_MODEL) + rows3(NA_WIDTH) + rows3(DF_WIDTH) + rows3(SSM_D_INNER) + [
            _mod_spec(2, fixed_row),
            _mod_spec(3, fixed_row),
            _mod_spec(4, fixed_row),
            _mod_spec(5, fixed_row),
            pl.BlockSpec((1, D_MODEL), lambda b, i: (0, 0)),
            const((D_MIX, D_MODEL)),
            const((D_MODEL, 2 * D_FF)),
            pl.BlockSpec((FFN_CONV, D_FF), lambda b, i: (0, 0)),
            pl.BlockSpec((1, D_FF), lambda b, i: (0, 0)),
            const((D_FF, D_MODEL)),
        ],
        out_specs=pl.BlockSpec((1, tm, D_MODEL), lambda b, i: (b, i, 0)),
        out_shape=jax.ShapeDtypeStruct((bsz, n, D_MODEL), F32),
        compiler_params=_cparams(("parallel", "parallel")),
        name="mlp",
    )(x, x, x, y_na, y_na, y_na, y_df, y_df, y_df, y_ssm, y_ssm, y_ssm, mod, mod, mod, mod, gain,
      w_out, w_up, conv_w, conv_b, w_down)


def kernel(x, c, ctx, c_ctx, w_ada, b_ada, g_mix, g_ffn, w_in, na_q_gain, na_k_gain, na_rpb, df_q_gain, df_k_gain, df_lambda, df_subln, ssm_conv_w, ssm_conv_b, ssm_dt_bias, ssm_a_log, ssm_d, ssm_norm, w_out, ffn_w_up, ffn_conv_w, ffn_conv_b, ffn_w_down):
    bsz, n, _ = x.shape
    ctx_row = bsz
    mod_rows = -(-(bsz + 1) // SUBLANE) * SUBLANE
    cc = jnp.concatenate([c, c_ctx[None, :], jnp.zeros((mod_rows - bsz - 1, D_MODEL), c.dtype)], axis=0)
    mods = _ada(cc.astype(F32), w_ada.astype(F32), b_ada.astype(F32))
    mods = mods.reshape(DEPTH, mod_rows, 1, 6 * D_MODEL)
    tables = _rope_tables(n)
    w_in_b = jnp.pad(w_in.astype(BF16), ((0, 0), (0, 0), (0, IN_COLS_PAD - IN_COLS)))
    w_out_b = w_out.astype(BF16)
    w_up_b = ffn_w_up.astype(BF16)
    w_down_b = ffn_w_down.astype(BF16)
    x = x.astype(F32)
    ctx = ctx.astype(F32)
    for l in range(DEPTH):
        ctx_out = l < DEPTH - 1
        mod = mods[l]
        gm = g_mix[l].astype(F32).reshape(1, D_MODEL)
        gf = g_ffn[l].astype(F32).reshape(1, D_MODEL)
        p = _inproj(x, mod, gm, w_in_b, l, None)
        pc = _inproj(ctx, mod, gm, w_in_b, l, ctx_row)
        y_na, yc_na = _na(p, pc, na_q_gain[l], na_k_gain[l], na_rpb[l], ctx_out)
        y_df, yc_df = _df(p, pc, tables, df_q_gain[l], df_k_gain[l], df_lambda[l], df_subln[l], l, ctx_out)
        y_ssm, yc_ssm = _ssd(p, pc, ssm_conv_w[l], ssm_conv_b[l], ssm_dt_bias[l], ssm_a_log[l], ssm_d[l],
                             ssm_norm[l], ctx_out)
        cw = ffn_conv_w[l].astype(F32)
        cb = ffn_conv_b[l].astype(F32).reshape(1, D_FF)
        x = _mlp(x, y_na, y_df, y_ssm, mod, gf, w_out_b, w_up_b, cw, cb, w_down_b, l, None)
        if ctx_out:
            ctx = _mlp(ctx, yc_na, yc_df, yc_ssm, mod, gf, w_out_b, w_up_b, cw, cb, w_down_b, l, ctx_row)
    return x
```
